```python
import jax, jax.numpy as jnp
from jax import lax
import numpy as np

D_MODEL = 2048
BATCH = 4
SEQ = 4096
DEPTH = 2

GRID_W = 64
CTX_LEN = 256
D_MIX = D_MODEL
RET_HEADS = 4
RET_DK = 128
RET_DV = 256
RET_QK_W = RET_HEADS * RET_DK
RET_W = RET_HEADS * RET_DV
RET_CHUNK = 128
CONV_W = 512
CONV_K = 31
NA_HEADS = 4
NA_DH = 128
NA_W = NA_HEADS * NA_DH
NA_ROWS = 8
NA_COLS = 16
D_FF = 5632
FFN_K = 3
D_IN = 2 * RET_QK_W + 2 * RET_W + 2 * CONV_W + 3 * NA_W
ROPE_BASE = 10000.0
EPS = 1e-6

kernel_name = "hybrid_retention_conformer_natten_dit"


def _rmsnorm(x, g):
    xf = x.astype(jnp.float32)
    y = xf * lax.rsqrt(jnp.mean(xf * xf, axis=-1, keepdims=True) + EPS)
    return (y * g.astype(jnp.float32)).astype(x.dtype)


def _layernorm(x, g, b):
    xf = x.astype(jnp.float32)
    mu = jnp.mean(xf, axis=-1, keepdims=True)
    var = jnp.mean(jnp.square(xf - mu), axis=-1, keepdims=True)
    y = (xf - mu) * lax.rsqrt(var + EPS)
    return (y * g.astype(jnp.float32) + b.astype(jnp.float32)).astype(x.dtype)


def _adaln(cond, w, b):
    m = jax.nn.silu(cond) @ w + b
    return jnp.split(m, 6, axis=-1)


def _modulate(h, shift, scale):
    return h * (1 + scale) + shift


def _depthwise_conv(x, w, b):
    y = lax.conv_general_dilated(x, w[:, None, :].astype(x.dtype), window_strides=(1,), padding="SAME",
                                 dimension_numbers=("NWC", "WIO", "NWC"), feature_group_count=x.shape[-1])
    return y + b.astype(x.dtype)


def _axial_rope(t):
    L, dh = t.shape[1], t.shape[-1]
    half = dh // 2
    nf = half // 2
    pos = jnp.arange(L)
    row = (pos // GRID_W).astype(jnp.float32)
    col = (pos % GRID_W).astype(jnp.float32)
    inv = ROPE_BASE ** (-jnp.arange(nf, dtype=jnp.float32) / nf)

    def rot(xa, p):
        ang = p[:, None] * inv[None, :]
        cos = jnp.cos(ang)[None, :, None, :]
        sin = jnp.sin(ang)[None, :, None, :]
        x1, x2 = xa[..., :nf], xa[..., nf:]
        return jnp.concatenate([x1 * cos - x2 * sin, x2 * cos + x1 * sin], axis=-1)

    return jnp.concatenate([rot(t[..., :half], row), rot(t[..., half:], col)], axis=-1)


def _retention_chunkwise(q, k, v, log_gamma, state0):
    b, L, h, _ = q.shape
    dv = v.shape[-1]
    n = L // RET_CHUNK

    def chunks(t):
        return t.reshape(b, n, RET_CHUNK, h, t.shape[-1]).transpose(1, 0, 3, 2, 4)

    idx = jnp.arange(RET_CHUNK, dtype=jnp.float32)
    diff = idx[:, None] - idx[None, :]
    lower = diff >= 0
    decay_in = jnp.where(lower, jnp.exp(jnp.where(lower, diff, 0.0) * log_gamma[:, None, None]), 0.0)
    xi = jnp.exp((idx + 1.0) * log_gamma[:, None])[None, :, :, None]
    zeta = jnp.exp((RET_CHUNK - 1.0 - idx) * log_gamma[:, None])[None, :, :, None]
    g_chunk = jnp.exp(RET_CHUNK * log_gamma)[None, :, None, None]

    def step(state, blk):
        qc, kc, vc = blk
        inner = jnp.einsum("bhid,bhjd->bhij", qc, kc) * decay_in[None]
        out = jnp.einsum("bhij,bhjv->bhiv", inner, vc) + jnp.einsum("bhid,bhdv->bhiv", qc, state) * xi
        state = state * g_chunk + jnp.einsum("bhjd,bhjv->bhdv", kc * zeta, vc)
        return state, out

    state, out = lax.scan(step, state0, (chunks(q), chunks(k), chunks(v)))
    return out.transpose(1, 0, 3, 2, 4).reshape(b, L, h, dv), state


def _bidirectional_retention(q_l, k_l, v_l, q_c, k_c, v_c, log_gamma):
    b = q_l.shape[0]
    zeros = jnp.zeros((b, RET_HEADS, RET_DK, RET_DV), jnp.float32)
    flip = lambda t: jnp.flip(t, axis=1)
    oc_f, s_f = _retention_chunkwise(q_c, k_c, v_c, log_gamma[0], zeros)
    ol_f, _ = _retention_chunkwise(q_l, k_l, v_l, log_gamma[0], s_f)
    oc_b, s_b = _retention_chunkwise(flip(q_c), flip(k_c), flip(v_c), log_gamma[1], zeros)
    ol_b, _ = _retention_chunkwise(flip(q_l), flip(k_l), flip(v_l), log_gamma[1], s_b)
    return ol_f + flip(ol_b), oc_f + flip(oc_b)


def _gated_group_norm(o, gate, g):
    b, L, h, dv = o.shape
    mu = jnp.mean(o, axis=-1, keepdims=True)
    var = jnp.mean(jnp.square(o - mu), axis=-1, keepdims=True)
    y = ((o - mu) * lax.rsqrt(var + EPS)).reshape(b, L, h * dv) * g.astype(jnp.float32)
    return y.astype(gate.dtype) * jax.nn.silu(gate)


def _retention_group(lq, lk, lv, lg, cq, ck, cv, cg, decay_logits, gn_g, with_ctx):
    def heads(t, d):
        return t.astype(jnp.float32).reshape(t.shape[0], t.shape[1], RET_HEADS, d)

    scale = RET_DK ** -0.5
    q_l = _axial_rope(heads(lq, RET_DK)) * scale
    k_l = _axial_rope(heads(lk, RET_DK))
    q_c = heads(cq, RET_DK) * scale
    k_c = heads(ck, RET_DK)
    log_gamma = jax.nn.log_sigmoid(decay_logits.astype(jnp.float32))
    o_l, o_c = _bidirectional_retention(q_l, k_l, heads(lv, RET_DV), q_c, k_c, heads(cv, RET_DV), log_gamma)
    out_l = _gated_group_norm(o_l, lg, gn_g)
    out_c = _gated_group_norm(o_c, cg, gn_g) if with_ctx else None
    return out_l, out_c


def _conv_group(a, b, dw_w, dw_b, ln_g, ln_b, pw):
    u = a * jax.nn.sigmoid(b)
    u = _depthwise_conv(u, dw_w, dw_b)
    u = jax.nn.silu(_layernorm(u, ln_g, ln_b))
    return u @ pw


def _na_latent(q, k, v, k_c, v_c, rpb):
    b, L, h, d = q.shape
    rows_n = L // GRID_W
    kh = min(NA_ROWS, rows_n)
    rows = jnp.arange(rows_n)
    cols = jnp.arange(GRID_W)
    key_rows = jnp.clip(rows - kh // 2, 0, rows_n - kh)[:, None] + jnp.arange(kh)[None, :]
    col_start = jnp.clip(cols - NA_COLS // 2, 0, GRID_W - NA_COLS)
    col_in = (cols[None, :] >= col_start[:, None]) & (cols[None, :] < col_start[:, None] + NA_COLS)
    row_off = key_rows - rows[:, None] + NA_ROWS - 1
    col_off = jnp.clip(cols[None, :] - cols[:, None] + NA_COLS - 1, 0, 2 * NA_COLS - 2)
    bias = rpb.astype(jnp.float32)[:, row_off[:, None, :, None], col_off[None, :, None, :]]
    qg = q.reshape(b, rows_n, GRID_W, h, d) * (NA_DH ** -0.5)
    kg = k.reshape(b, rows_n, GRID_W, h, d)[:, key_rows]
    vg = v.reshape(b, rows_n, GRID_W, h, d)[:, key_rows]
    s_lat = jnp.einsum("brqhd,brkwhd->bhrqkw", qg, kg).astype(jnp.float32) + bias[None]
    s_lat = jnp.where(col_in[:, None, :], s_lat, -jnp.inf)
    s_ctx = jnp.einsum("brqhd,bchd->bhrqc", qg, k_c).astype(jnp.float32)
    n_lat = kh * GRID_W
    p = jax.nn.softmax(jnp.concatenate([s_lat.reshape(b, h, rows_n, GRID_W, n_lat), s_ctx], axis=-1), axis=-1)
    p = p.astype(v.dtype)
    p_lat = p[..., :n_lat].reshape(b, h, rows_n, GRID_W, kh, GRID_W)
    out = jnp.einsum("bhrqkw,brkwhd->brqhd", p_lat, vg) + jnp.einsum("bhrqc,bchd->brqhd", p[..., n_lat:], v_c)
    return out.reshape(b, L, h * d)


def _na_context(q_c, k_c, v_c):
    b, lc, h, d = q_c.shape
    s = jnp.einsum("bqhd,bkhd->bhqk", q_c * (NA_DH ** -0.5), k_c).astype(jnp.float32)
    p = jax.nn.softmax(s, axis=-1).astype(v_c.dtype)
    return jnp.einsum("bhqk,bkhd->bqhd", p, v_c).reshape(b, lc, h * d)


def _token_mixers(p_l, p_c, decay_logits, gn_g, dw_w, dw_b, ln_g, ln_b, pw, rpb, with_ctx):
    sizes = [RET_QK_W, RET_QK_W, RET_W, RET_W, CONV_W, CONV_W, NA_W, NA_W, NA_W]
    cuts = [int(s) for s in np.cumsum(sizes)[:-1]]
    lq, lk, lv, lg, la, lb, nq, nk, nv = jnp.split(p_l, cuts, axis=-1)
    cq, ck, cv, cg, ca, cb, cnq, cnk, cnv = jnp.split(p_c, cuts, axis=-1)

    def na_heads(t):
        return t.reshape(t.shape[0], t.shape[1], NA_HEADS, NA_DH)

    ret_l, ret_c = _retention_group(lq, lk, lv, lg, cq, ck, cv, cg, decay_logits, gn_g, with_ctx)
    conv_l = _conv_group(la, lb, dw_w, dw_b, ln_g, ln_b, pw)
    k_c, v_c = na_heads(cnk), na_heads(cnv)
    na_l = _na_latent(na_heads(nq), na_heads(nk), na_heads(nv), k_c, v_c, rpb)
    out_l = jnp.concatenate([ret_l, conv_l, na_l], axis=-1)
    if not with_ctx:
        return out_l, None
    conv_c = _conv_group(ca, cb, dw_w, dw_b, ln_g, ln_b, pw)
    na_c = _na_context(na_heads(cnq), k_c, v_c)
    out_c = jnp.concatenate([ret_c, conv_c, na_c], axis=-1)
    return out_l, out_c


def _conv_ffn(h, up, dw_w, dw_b, down):
    u = _depthwise_conv(h @ up, dw_w, dw_b)
    val, gate = jnp.split(u, 2, axis=-1)
    return (jax.nn.silu(gate) * val) @ down


def setup_inputs(seed: int = 0) -> dict:
    key = jax.random.key(seed)
    ks = jax.random.split(key, 24)
    f32 = jnp.float32

    def nrm(k, shape, scale):
        return jax.random.normal(k, shape, f32) * scale

    base_decay = np.log(2.0 ** (5 + np.arange(RET_HEADS)) - 1.0)
    return {
        "x": nrm(ks[0], (BATCH, SEQ, D_MODEL), 1.0),
        "c": nrm(ks[1], (BATCH, D_MODEL), 1.0),
        "ctx": nrm(ks[2], (BATCH, CTX_LEN, D_MODEL), 1.0),
        "c_ctx": nrm(ks[3], (D_MODEL,), 1.0),
        "w_ada": nrm(ks[4], (DEPTH, D_MODEL, 6 * D_MODEL), 0.5 * D_MODEL ** -0.5),
        "b_ada": nrm(ks[5], (DEPTH, 6 * D_MODEL), 0.02),
        "norm1_g": 1.0 + nrm(ks[6], (DEPTH, D_MODEL), 0.02),
        "w_in": nrm(ks[7], (DEPTH, D_MODEL, D_IN), D_MODEL ** -0.5),
        "ret_decay": jnp.asarray(base_decay, f32)[None, None, :] + nrm(ks[8], (DEPTH, 2, RET_HEADS), 0.05),
        "ret_gn_g": 1.0 + nrm(ks[9], (DEPTH, RET_W), 0.02),
        "conv_dw_w": nrm(ks[10], (DEPTH, CONV_K, CONV_W), CONV_K ** -0.5),
        "conv_dw_b": nrm(ks[11], (DEPTH, CONV_W), 0.02),
        "conv_ln_g": 1.0 + nrm(ks[12], (DEPTH, CONV_W), 0.02),
        "conv_ln_b": nrm(ks[13], (DEPTH, CONV_W), 0.02),
        "conv_pw": nrm(ks[14], (DEPTH, CONV_W, CONV_W), CONV_W ** -0.5),
        "na_rpb": nrm(ks[15], (DEPTH, NA_HEADS, 2 * NA_ROWS - 1, 2 * NA_COLS - 1), 0.05),
        "w_out": nrm(ks[16], (DEPTH, D_MIX, D_MODEL), D_MIX ** -0.5),
        "norm2_g": 1.0 + nrm(ks[17], (DEPTH, D_MODEL), 0.02),
        "ffn_up": nrm(ks[18], (DEPTH, D_MODEL, 2 * D_FF), D_MODEL ** -0.5),
        "ffn_dw_w": nrm(ks[19], (DEPTH, FFN_K, 2 * D_FF), FFN_K ** -0.5),
        "ffn_dw_b": nrm(ks[20], (DEPTH, 2 * D_FF), 0.02),
        "ffn_down": nrm(ks[21], (DEPTH, D_FF, D_MODEL), D_FF ** -0.5),
        "final_g": 1.0 + nrm(ks[22], (D_MODEL,), 0.02),
    }


def reference(x, c, ctx, c_ctx, w_ada, b_ada, norm1_g, w_in, ret_decay, ret_gn_g, conv_dw_w, conv_dw_b,
              conv_ln_g, conv_ln_b, conv_pw, na_rpb, w_out, norm2_g, ffn_up, ffn_dw_w, ffn_dw_b, ffn_down, final_g):
    h_ctx = ctx
    for l in range(DEPTH):
        last = l == DEPTH - 1
        sh1, sc1, g1, sh2, sc2, g2 = [t[:, None, :] for t in _adaln(c, w_ada[l], b_ada[l])]
        csh1, csc1, cg1, csh2, csc2, cg2 = _adaln(c_ctx, w_ada[l], b_ada[l])
        hl = _modulate(_rmsnorm(x, norm1_g[l]), sh1, sc1)
        hc = _modulate(_rmsnorm(h_ctx, norm1_g[l]), csh1, csc1)
        mix_l, mix_c = _token_mixers(hl @ w_in[l], hc @ w_in[l], ret_decay[l], ret_gn_g[l], conv_dw_w[l],
                                     conv_dw_b[l], conv_ln_g[l], conv_ln_b[l], conv_pw[l], na_rpb[l],
                                     with_ctx=not last)
        x = x + g1 * (mix_l @ w_out[l])
        hl2 = _modulate(_rmsnorm(x, norm2_g[l]), sh2, sc2)
        x = x + g2 * _conv_ffn(hl2, ffn_up[l], ffn_dw_w[l], ffn_dw_b[l], ffn_down[l])
        if not last:
            h_ctx = h_ctx + cg1 * (mix_c @ w_out[l])
            hc2 = _modulate(_rmsnorm(h_ctx, norm2_g[l]), csh2, csc2)
            h_ctx = h_ctx + cg2 * _conv_ffn(hc2, ffn_up[l], ffn_dw_w[l], ffn_dw_b[l], ffn_down[l])
    return _rmsnorm(x, final_g)
```

```python
import functools

import numpy as np
import jax
import jax.numpy as jnp
from jax import lax
from jax.experimental import pallas as pl
from jax.experimental.pallas import tpu as pltpu

F32 = jnp.float32
BF16 = jnp.bfloat16

GRID_W = 64
RET_HEADS = 4
RET_DK = 128
RET_DV = 256
RET_CHUNK = 128
CONV_W = 512
CONV_K = 31
NA_HEADS = 4
NA_DH = 128
NA_ROWS = 8
NA_COLS = 16
FFN_K = 3
ROPE_BASE = 10000.0
EPS = 1e-6

RET_QK_W = RET_HEADS * RET_DK
RET_W = RET_HEADS * RET_DV
NA_W = NA_HEADS * NA_DH
OFF_RQ = 0
OFF_RK = OFF_RQ + RET_QK_W
OFF_RV = OFF_RK + RET_QK_W
OFF_RG = OFF_RV + RET_W
OFF_CA = OFF_RG + RET_W
OFF_CB = OFF_CA + CONV_W
OFF_NQ = OFF_CB + CONV_W
OFF_NK = OFF_NQ + NA_W
OFF_NV = OFF_NK + NA_W

V7X_VMEM_BYTES = 64 * 1024 * 1024
VMEM_LIMIT = V7X_VMEM_BYTES - 8 * 1024 * 1024
SUBLANES = 8
LANES = 128
COND_ROWS = 8
MASK_VALUE = -1e30
CONV_PAD = 16


def _sigmoid(x):
    return 1.0 / (1.0 + jnp.exp(-x))


def _params(*sem):
    return pltpu.CompilerParams(dimension_semantics=sem, vmem_limit_bytes=VMEM_LIMIT)


def _adaln_kernel(c_ref, w_ref, b_ref, o_ref):
    c = c_ref[...]
    s = (c * _sigmoid(c)).astype(BF16)
    o_ref[0] = jnp.dot(s, w_ref[0].astype(BF16), preferred_element_type=F32) + b_ref[0]


def _adaln(cond, w_ada, b_ada):
    depth, d, n = w_ada.shape
    tn = 1024
    return pl.pallas_call(
        _adaln_kernel,
        grid=(depth, n // tn),
        in_specs=[
            pl.BlockSpec((COND_ROWS, d), lambda l, j: (0, 0)),
            pl.BlockSpec((1, d, tn), lambda l, j: (l, 0, j)),
            pl.BlockSpec((1, 1, tn), lambda l, j: (l, 0, j)),
        ],
        out_specs=pl.BlockSpec((1, COND_ROWS, tn), lambda l, j: (l, 0, j)),
        out_shape=jax.ShapeDtypeStruct((depth, COND_ROWS, n), F32),
        compiler_params=_params("parallel", "parallel"),
        name="adaln",
    )(cond, w_ada, b_ada.reshape(depth, 1, n))


def _mod_spec(d, chunk, row):
    return pl.BlockSpec((1, 1, d), lambda b, i, j: (row(b), 0, chunk))


def _norm_mod_matmul_kernel(x_ref, g_ref, sh_ref, sc_ref, w_ref, o_ref, h_ref):
    @pl.when(pl.program_id(2) == 0)
    def _():
        x = x_ref[0]
        ms = jnp.mean(x * x, axis=-1, keepdims=True)
        y = x * lax.rsqrt(ms + EPS) * g_ref[...]
        h_ref[...] = (y * (1.0 + sc_ref[0]) + sh_ref[0]).astype(BF16)

    o_ref[0] = jnp.dot(h_ref[...], w_ref[...], preferred_element_type=F32).astype(o_ref.dtype)


def _norm_mod_matmul(x, g, mod, row, shift_chunk, scale_chunk, w, tn, name):
    b, l, d = x.shape
    n = w.shape[1]
    tm = min(l, 1024)
    return pl.pallas_call(
        _norm_mod_matmul_kernel,
        grid=(b, l // tm, n // tn),
        in_specs=[
            pl.BlockSpec((1, tm, d), lambda b, i, j: (b, i, 0)),
            pl.BlockSpec((1, d), lambda b, i, j: (0, 0)),
            _mod_spec(d, shift_chunk, row),
            _mod_spec(d, scale_chunk, row),
            pl.BlockSpec((d, tn), lambda b, i, j: (0, j)),
        ],
        out_specs=pl.BlockSpec((1, tm, tn), lambda b, i, j: (b, i, j)),
        out_shape=jax.ShapeDtypeStruct((b, l, n), BF16),
        scratch_shapes=[pltpu.VMEM((tm, d), BF16)],
        compiler_params=_params("parallel", "parallel", "arbitrary"),
        name=name,
    )(x, g.reshape(1, d), mod, mod, w)


def _matmul_residual_kernel(*refs, n_a):
    a_refs = refs[:n_a]
    w_ref, res_ref, gate_ref, o_ref = refs[n_a:]
    acc = None
    off = 0
    for a_ref in a_refs:
        k = a_ref.shape[-1]
        part = jnp.dot(a_ref[0], w_ref[off:off + k, :], preferred_element_type=F32)
        acc = part if acc is None else acc + part
        off += k
    o_ref[0] = res_ref[0] + gate_ref[0] * acc


def _matmul_residual(a_list, w, res, mod, row, gate_chunk, tn, name):
    b, l, n = res.shape
    k_total = w.shape[0]
    tm = min(l, 1024)
    in_specs = [pl.BlockSpec((1, tm, a.shape[-1]), lambda b, i, j: (b, i, 0)) for a in a_list]
    in_specs += [
        pl.BlockSpec((k_total, tn), lambda b, i, j: (0, j)),
        pl.BlockSpec((1, tm, tn), lambda b, i, j: (b, i, j)),
        pl.BlockSpec((1, 1, tn), lambda b, i, j: (row(b), 0, gate_chunk * (n // tn) + j)),
    ]
    return pl.pallas_call(
        functools.partial(_matmul_residual_kernel, n_a=len(a_list)),
        grid=(b, l // tm, n // tn),
        in_specs=in_specs,
        out_specs=pl.BlockSpec((1, tm, tn), lambda b, i, j: (b, i, j)),
        out_shape=jax.ShapeDtypeStruct((b, l, n), F32),
        compiler_params=_params("parallel", "parallel", "arbitrary"),
        name=name,
    )(*a_list, w, res, mod)


def _rope(t, cos, sin):
    lane = lax.broadcasted_iota(jnp.int32, t.shape, 1)
    partner = jnp.where((lane & 32) == 0, pltpu.roll(t, LANES - 32, 1), pltpu.roll(t, 32, 1))
    return t * cos + partner * sin


def _retention_kernel(lg_ref, ql_ref, kl_ref, vl_ref, gl_ref, qc_ref, kc_ref, vc_ref, gc_ref, cos_ref, sin_ref,
                      gn_ref, *rest, with_ctx, n_lat, n_ctx):
    if with_ctx:
        ol_ref, oc_ref, acc_ref, qr_ref, kr_ref, st_ref, accc_ref = rest
    else:
        ol_ref, acc_ref, qr_ref, kr_ref, st_ref = rest
    c = RET_CHUNK
    scale = RET_DK ** -0.5
    h = pl.program_id(1)
    lgf = lg_ref[0, h]
    lgb = lg_ref[1, h]
    ii = lax.broadcasted_iota(jnp.int32, (c, c), 0).astype(F32)
    jj = lax.broadcasted_iota(jnp.int32, (c, c), 1).astype(F32)
    diff = ii - jj
    decay = (jnp.where(diff >= 0, jnp.exp(jnp.maximum(diff, 0.0) * lgf), 0.0)
             + jnp.where(diff <= 0, jnp.exp(jnp.maximum(-diff, 0.0) * lgb), 0.0))
    ic = lax.broadcasted_iota(jnp.int32, (c, 1), 0).astype(F32)
    xi_f = jnp.exp((ic + 1.0) * lgf)
    zeta_f = jnp.exp((c - 1.0 - ic) * lgf)
    xi_b = jnp.exp((c - ic) * lgb)
    zeta_b = jnp.exp(ic * lgb)
    g_f = jnp.exp(jnp.full((1, RET_DV), float(c), F32) * lgf)
    g_b = jnp.exp(jnp.full((1, RET_DV), float(c), F32) * lgb)
    gn = gn_ref[...]

    def intra_and_inter(qb, kb, v, xi):
        s = lax.dot_general(qb, kb, (((1,), (1,)), ((), ())), preferred_element_type=F32)
        p = (s * decay).astype(BF16)
        inter = jnp.dot(qb, st_ref[...].astype(BF16), preferred_element_type=F32) * xi
        return jnp.dot(p, v, preferred_element_type=F32) + inter

    def update_state(k_f32, v, zeta, g):
        kz = (k_f32 * zeta).T.astype(BF16)
        st_ref[...] = st_ref[...] * g + jnp.dot(kz, v, preferred_element_type=F32)

    def finish(total, gate):
        mu = jnp.mean(total, axis=-1, keepdims=True)
        d = total - mu
        var = jnp.mean(d * d, axis=-1, keepdims=True)
        y = d * lax.rsqrt(var + EPS) * gn
        return (y * (gate * _sigmoid(gate))).astype(BF16)

    st_ref[...] = jnp.zeros_like(st_ref)
    for j in range(n_ctx):
        rows = slice(j * c, (j + 1) * c)
        q = qc_ref[0, rows, :].astype(F32) * scale
        k = kc_ref[0, rows, :].astype(F32)
        v = vc_ref[0, rows, :]
        if with_ctx:
            accc_ref[rows, :] = intra_and_inter(q.astype(BF16), k.astype(BF16), v, xi_f)
        update_state(k, v, zeta_f, g_f)

    def fwd_body(j, carry):
        r0 = pl.multiple_of(j * c, c)
        rows = pl.ds(r0, c)
        cos = cos_ref[rows, :]
        sin = sin_ref[rows, :]
        q = _rope(ql_ref[0, rows, :].astype(F32), cos, sin) * scale
        k = _rope(kl_ref[0, rows, :].astype(F32), cos, sin)
        qb = q.astype(BF16)
        kb = k.astype(BF16)
        qr_ref[rows, :] = qb
        kr_ref[rows, :] = kb
        v = vl_ref[0, rows, :]
        acc_ref[rows, :] = intra_and_inter(qb, kb, v, xi_f)
        update_state(k, v, zeta_f, g_f)
        return carry

    lax.fori_loop(0, n_lat, fwd_body, 0)

    st_ref[...] = jnp.zeros_like(st_ref)
    for j in reversed(range(n_ctx)):
        rows = slice(j * c, (j + 1) * c)
        k = kc_ref[0, rows, :].astype(F32)
        v = vc_ref[0, rows, :]
        if with_ctx:
            qb = (qc_ref[0, rows, :].astype(F32) * scale).astype(BF16)
            inter = jnp.dot(qb, st_ref[...].astype(BF16), preferred_element_type=F32) * xi_b
            oc_ref[0, rows, :] = finish(accc_ref[rows, :] + inter, gc_ref[0, rows, :].astype(F32))
        update_state(k, v, zeta_b, g_b)

    def bwd_body(jr, carry):
        r0 = pl.multiple_of((n_lat - 1 - jr) * c, c)
        rows = pl.ds(r0, c)
        qb = qr_ref[rows, :]
        v = vl_ref[0, rows, :]
        inter = jnp.dot(qb, st_ref[...].astype(BF16), preferred_element_type=F32) * xi_b
        ol_ref[0, rows, :] = finish(acc_ref[rows, :] + inter, gl_ref[0, rows, :].astype(F32))
        update_state(kr_ref[rows, :].astype(F32), v, zeta_b, g_b)
        return carry

    lax.fori_loop(0, n_lat, bwd_body, 0)


def _retention(p_l, p_c, log_gamma, gn_g, cos, sin, with_ctx):
    b, l, _ = p_l.shape
    lc = p_c.shape[1]
    c = RET_CHUNK

    def col(width, off, length):
        return pl.BlockSpec((1, length, width), lambda b, h: (b, 0, off // width + h))

    in_specs = [
        pl.BlockSpec(memory_space=pltpu.SMEM),
        col(RET_DK, OFF_RQ, l), col(RET_DK, OFF_RK, l), col(RET_DV, OFF_RV, l), col(RET_DV, OFF_RG, l),
        col(RET_DK, OFF_RQ, lc), col(RET_DK, OFF_RK, lc), col(RET_DV, OFF_RV, lc), col(RET_DV, OFF_RG, lc),
        pl.BlockSpec((l, RET_DK), lambda b, h: (0, 0)),
        pl.BlockSpec((l, RET_DK), lambda b, h: (0, 0)),
        pl.BlockSpec((1, RET_DV), lambda b, h: (0, h)),
    ]
    out_specs = [pl.BlockSpec((1, l, RET_DV), lambda b, h: (b, 0, h))]
    out_shape = [jax.ShapeDtypeStruct((b, l, RET_W), BF16)]
    scratch = [pltpu.VMEM((l, RET_DV), F32), pltpu.VMEM((l, RET_DK), BF16), pltpu.VMEM((l, RET_DK), BF16),
               pltpu.VMEM((RET_DK, RET_DV), F32)]
    if with_ctx:
        out_specs.append(pl.BlockSpec((1, lc, RET_DV), lambda b, h: (b, 0, h)))
        out_shape.append(jax.ShapeDtypeStruct((b, lc, RET_W), BF16))
        scratch.append(pltpu.VMEM((lc, RET_DV), F32))
    outs = pl.pallas_call(
        functools.partial(_retention_kernel, with_ctx=with_ctx, n_lat=l // c, n_ctx=lc // c),
        grid=(b, RET_HEADS),
        in_specs=in_specs,
        out_specs=out_specs,
        out_shape=out_shape,
        scratch_shapes=scratch,
        compiler_params=_params("parallel", "parallel"),
        name="retention_ctx" if with_ctx else "retention",
    )(log_gamma, p_l, p_l, p_l, p_l, p_c, p_c, p_c, p_c, cos, sin, gn_g.reshape(1, RET_W))
    return (outs[0], outs[1]) if with_ctx else (outs[0], None)


def _conv_kernel(a_ref, b_ref, dww_ref, dwb_ref, lng_ref, lnb_ref, pw_ref, o_ref, upad_ref, y_ref, act_ref, *, l):
    pad = CONV_PAD
    upad_ref[0:pad, :] = jnp.zeros((pad, CONV_W), F32)
    upad_ref[pad + l:pad + l + pad, :] = jnp.zeros((pad, CONV_W), F32)
    tg = min(l, 256)

    def glu_body(t, carry):
        r0 = pl.multiple_of(t * tg, tg)
        a = a_ref[0, pl.ds(r0, tg), :].astype(F32)
        b = b_ref[0, pl.ds(r0, tg), :].astype(F32)
        upad_ref[pl.ds(pad + r0, tg), :] = a * _sigmoid(b)
        return carry

    lax.fori_loop(0, l // tg, glu_body, 0)

    tile = 64
    halo = 2 * pad
    lng = lng_ref[...]
    lnb = lnb_ref[...]

    def conv_body(t, carry):
        r0 = pl.multiple_of(t * tile, tile)
        for cb in range(CONV_W // LANES):
            cols = slice(cb * LANES, (cb + 1) * LANES)
            win = upad_ref[pl.ds(r0, tile + halo), cols]
            acc = jnp.zeros((tile, LANES), F32) + dwb_ref[:, cols]
            for r in range(SUBLANES):
                shifted = win[r:r + tile + halo - SUBLANES]
                for a in range(halo // SUBLANES):
                    k = SUBLANES * a + r - 1
                    if 0 <= k < CONV_K:
                        acc = acc + shifted[SUBLANES * a:SUBLANES * a + tile] * dww_ref[k:k + 1, cols]
            y_ref[:, cols] = acc
        y = y_ref[...]
        mu = jnp.mean(y, axis=-1, keepdims=True)
        d = y - mu
        var = jnp.mean(d * d, axis=-1, keepdims=True)
        u = d * lax.rsqrt(var + EPS) * lng + lnb
        act_ref[pl.ds(r0, tile), :] = (u * _sigmoid(u)).astype(BF16)
        return carry

    lax.fori_loop(0, l // tile, conv_body, 0)

    tmm = min(l, 512)

    def mm_body(t, carry):
        r0 = pl.multiple_of(t * tmm, tmm)
        o_ref[0, pl.ds(r0, tmm), :] = jnp.dot(act_ref[pl.ds(r0, tmm), :], pw_ref[...],
                                              preferred_element_type=F32).astype(BF16)
        return carry

    lax.fori_loop(0, l // tmm, mm_body, 0)


def _conv_module(p, dw_w, dw_b, ln_g, ln_b, pw):
    b, l, _ = p.shape
    vec = pl.BlockSpec((1, CONV_W), lambda b: (0, 0))
    return pl.pallas_call(
        functools.partial(_conv_kernel, l=l),
        grid=(b,),
        in_specs=[
            pl.BlockSpec((1, l, CONV_W), lambda b: (b, 0, OFF_CA // CONV_W)),
            pl.BlockSpec((1, l, CONV_W), lambda b: (b, 0, OFF_CB // CONV_W)),
            pl.BlockSpec((CONV_K, CONV_W), lambda b: (0, 0)),
            vec, vec, vec,
            pl.BlockSpec((CONV_W, CONV_W), lambda b: (0, 0)),
        ],
        out_specs=pl.BlockSpec((1, l, CONV_W), lambda b: (b, 0, 0)),
        out_shape=jax.ShapeDtypeStruct((b, l, CONV_W), BF16),
        scratch_shapes=[pltpu.VMEM((l + 2 * CONV_PAD, CONV_W), F32), pltpu.VMEM((64, CONV_W), F32),
                        pltpu.VMEM((l, CONV_W), BF16)],
        compiler_params=_params("parallel"),
        name="conv_module",
    )(p, p, dw_w, dw_b.reshape(1, CONV_W), ln_g.reshape(1, CONV_W), ln_b.reshape(1, CONV_W), pw)


def _na_bias_table(rpb, kh):
    cols = np.arange(GRID_W)
    col_start = np.clip(cols - NA_COLS // 2, 0, GRID_W - NA_COLS)
    col_in = (cols[None, :] >= col_start[:, None]) & (cols[None, :] < col_start[:, None] + NA_COLS)
    col_off = np.clip(cols[None, :] - cols[:, None] + NA_COLS - 1, 0, 2 * NA_COLS - 2)
    row_off = np.clip(np.arange(NA_ROWS)[:, None] + np.arange(kh)[None, :], 0, 2 * NA_ROWS - 2)
    t = rpb.astype(F32)[:, row_off[:, None, :, None], col_off[None, :, None, :]]
    t = jnp.where(col_in[None, None, :, None, :], t, MASK_VALUE)
    return t.reshape(rpb.shape[0], NA_ROWS, GRID_W, kh * GRID_W)


def _na_kernel(q_ref, k_ref, v_ref, kc_ref, vc_ref, bias_ref, o_ref, *, rows_n, kh):
    scale = NA_DH ** -0.5
    kc = kc_ref[0]
    vc = vc_ref[0]
    nt = (((1,), (1,)), ((), ()))

    def body(r, carry):
        s0 = jnp.clip(r - kh // 2, 0, rows_n - kh)
        sidx = s0 - r + (NA_ROWS - 1)
        q0 = pl.multiple_of(r * GRID_W, GRID_W)
        k0 = pl.multiple_of(s0 * GRID_W, GRID_W)
        q = q_ref[0, pl.ds(q0, GRID_W), :]
        kw = k_ref[0, pl.ds(k0, kh * GRID_W), :]
        vw = v_ref[0, pl.ds(k0, kh * GRID_W), :]
        s_lat = lax.dot_general(q, kw, nt, preferred_element_type=F32) * scale + bias_ref[0, sidx]
        s_ctx = lax.dot_general(q, kc, nt, preferred_element_type=F32) * scale
        m = jnp.maximum(jnp.max(s_lat, axis=-1, keepdims=True), jnp.max(s_ctx, axis=-1, keepdims=True))
        e_lat = jnp.exp(s_lat - m)
        e_ctx = jnp.exp(s_ctx - m)
        den = jnp.sum(e_lat, axis=-1, keepdims=True) + jnp.sum(e_ctx, axis=-1, keepdims=True)
        o = (jnp.dot(e_lat.astype(BF16), vw, preferred_element_type=F32)
             + jnp.dot(e_ctx.astype(BF16), vc, preferred_element_type=F32))
        o_ref[0, pl.ds(q0, GRID_W), :] = (o / den).astype(BF16)
        return carry

    lax.fori_loop(0, rows_n, body, 0)


def _na_latent(p_l, p_c, bias):
    b, l, _ = p_l.shape
    lc = p_c.shape[1]
    rows_n = l // GRID_W
    kh = min(NA_ROWS, rows_n)

    def col(off, length):
        return pl.BlockSpec((1, length, NA_DH), lambda b, h: (b, 0, off // NA_DH + h))

    return pl.pallas_call(
        functools.partial(_na_kernel, rows_n=rows_n, kh=kh),
        grid=(b, NA_HEADS),
        in_specs=[col(OFF_NQ, l), col(OFF_NK, l), col(OFF_NV, l), col(OFF_NK, lc), col(OFF_NV, lc),
                  pl.BlockSpec((1, NA_ROWS, GRID_W, kh * GRID_W), lambda b, h: (h, 0, 0, 0))],
        out_specs=pl.BlockSpec((1, l, NA_DH), lambda b, h: (b, 0, h)),
        out_shape=jax.ShapeDtypeStruct((b, l, NA_W), BF16),
        compiler_params=_params("parallel", "parallel"),
        name="na_latent",
    )(p_l, p_l, p_l, p_c, p_c, bias)


def _na_ctx_kernel(q_ref, k_ref, v_ref, o_ref):
    scale = NA_DH ** -0.5
    s = lax.dot_general(q_ref[0], k_ref[0], (((1,), (1,)), ((), ())), preferred_element_type=F32) * scale
    e = jnp.exp(s - jnp.max(s, axis=-1, keepdims=True))
    o = jnp.dot(e.astype(BF16), v_ref[0], preferred_element_type=F32)
    o_ref[0] = (o / jnp.sum(e, axis=-1, keepdims=True)).astype(BF16)


def _na_context(p_c):
    b, lc, _ = p_c.shape

    def col(off):
        return pl.BlockSpec((1, lc, NA_DH), lambda b, h: (b, 0, off // NA_DH + h))

    return pl.pallas_call(
        _na_ctx_kernel,
        grid=(b, NA_HEADS),
        in_specs=[col(OFF_NQ), col(OFF_NK), col(OFF_NV)],
        out_specs=pl.BlockSpec((1, lc, NA_DH), lambda b, h: (b, 0, h)),
        out_shape=jax.ShapeDtypeStruct((b, lc, NA_W), BF16),
        compiler_params=_params("parallel", "parallel"),
        name="na_context",
    )(p_c, p_c, p_c)


def _ffn_act_kernel(val_ref, gate_ref, vprev_ref, gprev_ref, vnext_ref, gnext_ref, wv_ref, wg_ref, bv_ref, bg_ref,
                    o_ref, *, halo_rows):
    i = pl.program_id(1)
    last = pl.num_programs(1) - 1
    t = val_ref.shape[1]
    row = lax.broadcasted_iota(jnp.int32, (t, 1), 0)

    def conv3(x_ref, prev_ref, next_ref, w_ref, b_ref):
        x = x_ref[0].astype(F32)
        prev_row = jnp.where(i == 0, 0.0, prev_ref[0, halo_rows - 1:halo_rows, :].astype(F32))
        next_row = jnp.where(i == last, 0.0, next_ref[0, 0:1, :].astype(F32))
        x_prev = jnp.where(row == 0, prev_row, pltpu.roll(x, 1, 0))
        x_next = jnp.where(row == t - 1, next_row, pltpu.roll(x, t - 1, 0))
        return x_prev * w_ref[0:1, :] + x * w_ref[1:2, :] + x_next * w_ref[2:3, :] + b_ref[...]

    val = conv3(val_ref, vprev_ref, vnext_ref, wv_ref, bv_ref)
    gate = conv3(gate_ref, gprev_ref, gnext_ref, wg_ref, bg_ref)
    o_ref[0] = (gate * _sigmoid(gate) * val).astype(BF16)


def _ffn_act(u, dw_w, dw_b):
    b, l, f2 = u.shape
    f = f2 // 2
    t = min(l, 512)
    tc = 512
    nc = f // tc
    hr = 16
    nrb = l // hr

    def main(off):
        return pl.BlockSpec((1, t, tc), lambda b, i, j: (b, i, off + j))

    def prev(off):
        return pl.BlockSpec((1, hr, tc), lambda b, i, j: (b, jnp.maximum(i * (t // hr) - 1, 0), off + j))

    def nxt(off):
        return pl.BlockSpec((1, hr, tc), lambda b, i, j: (b, jnp.minimum((i + 1) * (t // hr), nrb - 1), off + j))

    def wspec(rows, off):
        return pl.BlockSpec((rows, tc), lambda b, i, j: (0, off + j))

    return pl.pallas_call(
        functools.partial(_ffn_act_kernel, halo_rows=hr),
        grid=(b, l // t, nc),
        in_specs=[main(0), main(nc), prev(0), prev(nc), nxt(0), nxt(nc),
                  wspec(FFN_K, 0), wspec(FFN_K, nc), wspec(1, 0), wspec(1, nc)],
        out_specs=pl.BlockSpec((1, t, tc), lambda b, i, j: (b, i, j)),
        out_shape=jax.ShapeDtypeStruct((b, l, f), BF16),
        compiler_params=_params("parallel", "parallel", "parallel"),
        name="ffn_act",
    )(u, u, u, u, u, u, dw_w, dw_w, dw_b.reshape(1, f2), dw_b.reshape(1, f2))


def _rmsnorm_kernel(x_ref, g_ref, o_ref):
    x = x_ref[0]
    ms = jnp.mean(x * x, axis=-1, keepdims=True)
    o_ref[0] = x * lax.rsqrt(ms + EPS) * g_ref[...]


def _rmsnorm(x, g):
    b, l, d = x.shape
    tm = min(l, 512)
    return pl.pallas_call(
        _rmsnorm_kernel,
        grid=(b, l // tm),
        in_specs=[pl.BlockSpec((1, tm, d), lambda b, i: (b, i, 0)), pl.BlockSpec((1, d), lambda b, i: (0, 0))],
        out_specs=pl.BlockSpec((1, tm, d), lambda b, i: (b, i, 0)),
        out_shape=jax.ShapeDtypeStruct((b, l, d), F32),
        compiler_params=_params("parallel", "parallel"),
        name="final_rmsnorm",
    )(x, g.reshape(1, d))


def _rope_tables(l):
    nf = RET_DK // 4
    pos = np.arange(l)
    inv = ROPE_BASE ** (-jnp.arange(nf, dtype=F32) / nf)
    ang_r = jnp.asarray(pos // GRID_W, F32)[:, None] * inv[None, :]
    ang_c = jnp.asarray(pos % GRID_W, F32)[:, None] * inv[None, :]
    cos = jnp.concatenate([jnp.cos(ang_r), jnp.cos(ang_r), jnp.cos(ang_c), jnp.cos(ang_c)], axis=-1)
    sin = jnp.concatenate([-jnp.sin(ang_r), jnp.sin(ang_r), -jnp.sin(ang_c), jnp.sin(ang_c)], axis=-1)
    return cos, sin


def kernel(x, c, ctx, c_ctx, w_ada, b_ada, norm1_g, w_in, ret_decay, ret_gn_g, conv_dw_w, conv_dw_b, conv_ln_g,
           conv_ln_b, conv_pw, na_rpb, w_out, norm2_g, ffn_up, ffn_dw_w, ffn_dw_b, ffn_down, final_g):
    depth = w_ada.shape[0]
    bsz, l, d = x.shape
    rows_n = l // GRID_W
    kh = min(NA_ROWS, rows_n)

    cond = jnp.zeros((COND_ROWS, d), F32).at[:bsz].set(c).at[bsz].set(c_ctx)
    mod_all = _adaln(cond, w_ada, b_ada).reshape(depth, COND_ROWS, 1, 6 * d)
    lat_row = lambda b: b
    ctx_row = lambda b: bsz
    cos, sin = _rope_tables(l)
    log_gamma = jax.nn.log_sigmoid(ret_decay.astype(F32))

    h_ctx = ctx
    for layer in range(depth):
        last = layer == depth - 1
        mod = mod_all[layer]
        w_in_b = w_in[layer].astype(BF16)
        w_out_b = w_out[layer].astype(BF16)
        up_b = ffn_up[layer].astype(BF16)
        down_b = ffn_down[layer].astype(BF16)
        pw_b = conv_pw[layer].astype(BF16)
        bias = _na_bias_table(na_rpb[layer], kh)

        p_l = _norm_mod_matmul(x, norm1_g[layer], mod, lat_row, 0, 1, w_in_b, 512, "in_proj")
        p_c = _norm_mod_matmul(h_ctx, norm1_g[layer], mod, ctx_row, 0, 1, w_in_b, 512, "in_proj_ctx")

        ret_l, ret_c = _retention(p_l, p_c, log_gamma[layer], ret_gn_g[layer], cos, sin, with_ctx=not last)
        conv_l = _conv_module(p_l, conv_dw_w[layer], conv_dw_b[layer], conv_ln_g[layer], conv_ln_b[layer], pw_b)
        na_l = _na_latent(p_l, p_c, bias)
        x = _matmul_residual([ret_l, conv_l, na_l], w_out_b, x, mod, lat_row, 2, 512, "out_proj")

        u = _norm_mod_matmul(x, norm2_g[layer], mod, lat_row, 3, 4, up_b, 512, "ffn_up")
        act = _ffn_act(u, ffn_dw_w[layer], ffn_dw_b[layer])
        x = _matmul_residual([act], down_b, x, mod, lat_row, 5, 512, "ffn_down")

        if not last:
            conv_c = _conv_module(p_c, conv_dw_w[layer], conv_dw_b[layer], conv_ln_g[layer], conv_ln_b[layer], pw_b)
            na_c = _na_context(p_c)
            h_ctx = _matmul_residual([ret_c, conv_c, na_c], w_out_b, h_ctx, mod, ctx_row, 2, 512, "out_proj_ctx")
            u_c = _norm_mod_matmul(h_ctx, norm2_g[layer], mod, ctx_row, 3, 4, up_b, 512, "ffn_up_ctx")
            act_c = _ffn_act(u_c, ffn_dw_w[layer], ffn_dw_b[layer])
            h_ctx = _matmul_residual([act_c], down_b, h_ctx, mod, ctx_row, 5, 512, "ffn_down_ctx")

    return _rmsnorm(x, final_g)
```

```python
import functools

import numpy as np
import jax
import jax.numpy as jnp
from jax import lax
from jax.experimental import pallas as pl
from jax.experimental.pallas import tpu as pltpu

F32 = jnp.float32
BF16 = jnp.bfloat16

GRID_W = 64
RET_HEADS = 4
RET_DK = 128
RET_DV = 256
RET_CHUNK = 128
CONV_W = 512
CONV_K = 31
NA_HEADS = 4
NA_DH = 128
NA_ROWS = 8
NA_COLS = 16
FFN_K = 3
ROPE_BASE = 10000.0
EPS = 1e-6

RET_QK_W = RET_HEADS * RET_DK
RET_W = RET_HEADS * RET_DV
NA_W = NA_HEADS * NA_DH
OFF_RQ = 0
OFF_RK = OFF_RQ + RET_QK_W
OFF_RV = OFF_RK + RET_QK_W
OFF_RG = OFF_RV + RET_W
OFF_CA = OFF_RG + RET_W
OFF_CB = OFF_CA + CONV_W
OFF_NQ = OFF_CB + CONV_W
OFF_NK = OFF_NQ + NA_W
OFF_NV = OFF_NK + NA_W

V7X_VMEM_BYTES = 64 * 1024 * 1024
VMEM_LIMIT = V7X_VMEM_BYTES - 8 * 1024 * 1024
SUBLANES = 8
LANES = 128
COND_ROWS = 8
MASK_VALUE = -1e30
CONV_PAD = 16


def _sigmoid(x):
    return 1.0 / (1.0 + jnp.exp(-x))


def _params(*sem):
    return pltpu.CompilerParams(dimension_semantics=sem, vmem_limit_bytes=VMEM_LIMIT)


def _adaln_kernel(c_ref, w_ref, b_ref, o_ref):
    c = c_ref[...]
    s = (c * _sigmoid(c)).astype(BF16)
    o_ref[0] = jnp.dot(s, w_ref[0].astype(BF16), preferred_element_type=F32) + b_ref[0]


def _adaln(cond, w_ada, b_ada):
    depth, d, n = w_ada.shape
    tn = 1024
    return pl.pallas_call(
        _adaln_kernel,
        grid=(depth, n // tn),
        in_specs=[
            pl.BlockSpec((COND_ROWS, d), lambda l, j: (0, 0)),
            pl.BlockSpec((1, d, tn), lambda l, j: (l, 0, j)),
            pl.BlockSpec((1, 1, tn), lambda l, j: (l, 0, j)),
        ],
        out_specs=pl.BlockSpec((1, COND_ROWS, tn), lambda l, j: (l, 0, j)),
        out_shape=jax.ShapeDtypeStruct((depth, COND_ROWS, n), F32),
        compiler_params=_params("parallel", "parallel"),
        name="adaln",
    )(cond, w_ada, b_ada.reshape(depth, 1, n))


def _mod_spec(layer, d, chunk, row):
    return pl.BlockSpec((None, 1, 1, d), lambda b, i, j: (layer, row(b), 0, chunk))


def _norm_mod_matmul_kernel(x_ref, g_ref, sh_ref, sc_ref, w_ref, o_ref, h_ref):
    @pl.when(pl.program_id(2) == 0)
    def _():
        x = x_ref[0]
        ms = jnp.mean(x * x, axis=-1, keepdims=True)
        y = x * lax.rsqrt(ms + EPS) * g_ref[...]
        h_ref[...] = (y * (1.0 + sc_ref[0]) + sh_ref[0]).astype(BF16)

    o_ref[0] = jnp.dot(h_ref[...], w_ref[...], preferred_element_type=F32).astype(o_ref.dtype)


def _norm_mod_matmul(x, g, mod, layer, row, shift_chunk, scale_chunk, w, tn, name):
    b, l, d = x.shape
    n = w.shape[-1]
    tm = min(l, 1024)
    return pl.pallas_call(
        _norm_mod_matmul_kernel,
        grid=(b, l // tm, n // tn),
        in_specs=[
            pl.BlockSpec((1, tm, d), lambda b, i, j: (b, i, 0)),
            pl.BlockSpec((1, d), lambda b, i, j: (0, 0)),
            _mod_spec(layer, d, shift_chunk, row),
            _mod_spec(layer, d, scale_chunk, row),
            pl.BlockSpec((None, d, tn), lambda b, i, j: (layer, 0, j)),
        ],
        out_specs=pl.BlockSpec((1, tm, tn), lambda b, i, j: (b, i, j)),
        out_shape=jax.ShapeDtypeStruct((b, l, n), BF16),
        scratch_shapes=[pltpu.VMEM((tm, d), BF16)],
        compiler_params=_params("parallel", "parallel", "arbitrary"),
        name=name,
    )(x, g.reshape(1, d), mod, mod, w)


def _matmul_residual_kernel(*refs, n_a):
    a_refs = refs[:n_a]
    w_ref, res_ref, gate_ref, o_ref = refs[n_a:]
    acc = None
    off = 0
    for a_ref in a_refs:
        k = a_ref.shape[-1]
        part = jnp.dot(a_ref[0], w_ref[off:off + k, :], preferred_element_type=F32)
        acc = part if acc is None else acc + part
        off += k
    o_ref[0] = res_ref[0] + gate_ref[0] * acc


def _matmul_residual(a_list, w, res, mod, layer, row, gate_chunk, tn, name):
    b, l, n = res.shape
    k_total = w.shape[1]
    tm = min(l, 1024)
    tn = min(tn, n)
    in_specs = [pl.BlockSpec((1, tm, a.shape[-1]), lambda b, i, j: (b, i, 0)) for a in a_list]
    in_specs += [
        pl.BlockSpec((None, k_total, tn), lambda b, i, j: (layer, 0, j)),
        pl.BlockSpec((1, tm, tn), lambda b, i, j: (b, i, j)),
        pl.BlockSpec((None, 1, 1, tn), lambda b, i, j: (layer, row(b), 0, gate_chunk * (n // tn) + j)),
    ]
    return pl.pallas_call(
        functools.partial(_matmul_residual_kernel, n_a=len(a_list)),
        grid=(b, l // tm, n // tn),
        in_specs=in_specs,
        out_specs=pl.BlockSpec((1, tm, tn), lambda b, i, j: (b, i, j)),
        out_shape=jax.ShapeDtypeStruct((b, l, n), F32),
        compiler_params=_params("parallel", "parallel", "arbitrary"),
        name=name,
    )(*a_list, w, res, mod)


def _rope(t, cos, sin):
    lane = lax.broadcasted_iota(jnp.int32, t.shape, 1)
    partner = jnp.where((lane & 32) == 0, pltpu.roll(t, LANES - 32, 1), pltpu.roll(t, 32, 1))
    return t * cos + partner * sin


def _retention_kernel(lg_ref, ql_ref, kl_ref, vl_ref, gl_ref, qc_ref, kc_ref, vc_ref, gc_ref, cos_ref, sin_ref,
                      gn_ref, *rest, with_ctx, n_lat, n_ctx):
    if with_ctx:
        ol_ref, oc_ref, acc_ref, qr_ref, kr_ref, st_ref, accc_ref = rest
    else:
        ol_ref, acc_ref, qr_ref, kr_ref, st_ref = rest
    c = RET_CHUNK
    scale = RET_DK ** -0.5
    h = pl.program_id(1)
    lgf = lg_ref[0, h]
    lgb = lg_ref[1, h]
    ii = lax.broadcasted_iota(jnp.int32, (c, c), 0).astype(F32)
    jj = lax.broadcasted_iota(jnp.int32, (c, c), 1).astype(F32)
    diff = ii - jj
    decay = (jnp.where(diff >= 0, jnp.exp(jnp.maximum(diff, 0.0) * lgf), 0.0)
             + jnp.where(diff <= 0, jnp.exp(jnp.maximum(-diff, 0.0) * lgb), 0.0))
    ic = lax.broadcasted_iota(jnp.int32, (c, 1), 0).astype(F32)
    xi_f = jnp.exp((ic + 1.0) * lgf)
    zeta_f = jnp.exp((c - 1.0 - ic) * lgf)
    xi_b = jnp.exp((c - ic) * lgb)
    zeta_b = jnp.exp(ic * lgb)
    g_f = jnp.exp(jnp.full((1, RET_DV), float(c), F32) * lgf)
    g_b = jnp.exp(jnp.full((1, RET_DV), float(c), F32) * lgb)
    gn = gn_ref[...]

    def intra_and_inter(qb, kb, v, xi):
        s = lax.dot_general(qb, kb, (((1,), (1,)), ((), ())), preferred_element_type=F32)
        p = (s * decay).astype(BF16)
        inter = jnp.dot(qb, st_ref[...].astype(BF16), preferred_element_type=F32) * xi
        return jnp.dot(p, v, preferred_element_type=F32) + inter

    def update_state(k_f32, v, zeta, g):
        kz = (k_f32 * zeta).T.astype(BF16)
        st_ref[...] = st_ref[...] * g + jnp.dot(kz, v, preferred_element_type=F32)

    def finish(total, gate):
        mu = jnp.mean(total, axis=-1, keepdims=True)
        d = total - mu
        var = jnp.mean(d * d, axis=-1, keepdims=True)
        y = d * lax.rsqrt(var + EPS) * gn
        return (y * (gate * _sigmoid(gate))).astype(BF16)

    st_ref[...] = jnp.zeros_like(st_ref)
    for j in range(n_ctx):
        rows = slice(j * c, (j + 1) * c)
        q = qc_ref[0, rows, :].astype(F32) * scale
        k = kc_ref[0, rows, :].astype(F32)
        v = vc_ref[0, rows, :]
        if with_ctx:
            accc_ref[rows, :] = intra_and_inter(q.astype(BF16), k.astype(BF16), v, xi_f)
        update_state(k, v, zeta_f, g_f)

    def fwd_body(j, carry):
        r0 = pl.multiple_of(j * c, c)
        rows = pl.ds(r0, c)
        cos = cos_ref[rows, :]
        sin = sin_ref[rows, :]
        q = _rope(ql_ref[0, rows, :].astype(F32), cos, sin) * scale
        k = _rope(kl_ref[0, rows, :].astype(F32), cos, sin)
        qb = q.astype(BF16)
        kb = k.astype(BF16)
        qr_ref[rows, :] = qb
        kr_ref[rows, :] = kb
        v = vl_ref[0, rows, :]
        acc_ref[rows, :] = intra_and_inter(qb, kb, v, xi_f)
        update_state(k, v, zeta_f, g_f)
        return carry

    lax.fori_loop(0, n_lat, fwd_body, 0)

    st_ref[...] = jnp.zeros_like(st_ref)
    for j in reversed(range(n_ctx)):
        rows = slice(j * c, (j + 1) * c)
        k = kc_ref[0, rows, :].astype(F32)
        v = vc_ref[0, rows, :]
        if with_ctx:
            qb = (qc_ref[0, rows, :].astype(F32) * scale).astype(BF16)
            inter = jnp.dot(qb, st_ref[...].astype(BF16), preferred_element_type=F32) * xi_b
            oc_ref[0, rows, :] = finish(accc_ref[rows, :] + inter, gc_ref[0, rows, :].astype(F32))
        update_state(k, v, zeta_b, g_b)

    def bwd_body(jr, carry):
        r0 = pl.multiple_of((n_lat - 1 - jr) * c, c)
        rows = pl.ds(r0, c)
        qb = qr_ref[rows, :]
        v = vl_ref[0, rows, :]
        inter = jnp.dot(qb, st_ref[...].astype(BF16), preferred_element_type=F32) * xi_b
        ol_ref[0, rows, :] = finish(acc_ref[rows, :] + inter, gl_ref[0, rows, :].astype(F32))
        update_state(kr_ref[rows, :].astype(F32), v, zeta_b, g_b)
        return carry

    lax.fori_loop(0, n_lat, bwd_body, 0)


def _retention(p_l, p_c, log_gamma, gn_g, cos, sin, with_ctx):
    b, l, _ = p_l.shape
    lc = p_c.shape[1]
    c = RET_CHUNK

    def col(width, off, length):
        return pl.BlockSpec((1, length, width), lambda b, h: (b, 0, off // width + h))

    in_specs = [
        pl.BlockSpec(memory_space=pltpu.SMEM),
        col(RET_DK, OFF_RQ, l), col(RET_DK, OFF_RK, l), col(RET_DV, OFF_RV, l), col(RET_DV, OFF_RG, l),
        col(RET_DK, OFF_RQ, lc), col(RET_DK, OFF_RK, lc), col(RET_DV, OFF_RV, lc), col(RET_DV, OFF_RG, lc),
        pl.BlockSpec((l, RET_DK), lambda b, h: (0, 0)),
        pl.BlockSpec((l, RET_DK), lambda b, h: (0, 0)),
        pl.BlockSpec((1, RET_DV), lambda b, h: (0, h)),
    ]
    out_specs = [pl.BlockSpec((1, l, RET_DV), lambda b, h: (b, 0, h))]
    out_shape = [jax.ShapeDtypeStruct((b, l, RET_W), BF16)]
    scratch = [pltpu.VMEM((l, RET_DV), F32), pltpu.VMEM((l, RET_DK), BF16), pltpu.VMEM((l, RET_DK), BF16),
               pltpu.VMEM((RET_DK, RET_DV), F32)]
    if with_ctx:
        out_specs.append(pl.BlockSpec((1, lc, RET_DV), lambda b, h: (b, 0, h)))
        out_shape.append(jax.ShapeDtypeStruct((b, lc, RET_W), BF16))
        scratch.append(pltpu.VMEM((lc, RET_DV), F32))
    outs = pl.pallas_call(
        functools.partial(_retention_kernel, with_ctx=with_ctx, n_lat=l // c, n_ctx=lc // c),
        grid=(b, RET_HEADS),
        in_specs=in_specs,
        out_specs=out_specs,
        out_shape=out_shape,
        scratch_shapes=scratch,
        compiler_params=_params("parallel", "parallel"),
        name="retention_ctx" if with_ctx else "retention",
    )(log_gamma, p_l, p_l, p_l, p_l, p_c, p_c, p_c, p_c, cos, sin, gn_g.reshape(1, RET_W))
    return (outs[0], outs[1]) if with_ctx else (outs[0], None)


def _conv_kernel(a_ref, b_ref, dww_ref, dwb_ref, lng_ref, lnb_ref, pw_ref, o_ref, upad_ref, y_ref, act_ref, *, l):
    pad = CONV_PAD
    upad_ref[0:pad, :] = jnp.zeros((pad, CONV_W), F32)
    upad_ref[pad + l:pad + l + pad, :] = jnp.zeros((pad, CONV_W), F32)
    tg = min(l, 256)

    def glu_body(t, carry):
        r0 = pl.multiple_of(t * tg, tg)
        a = a_ref[0, pl.ds(r0, tg), :].astype(F32)
        b = b_ref[0, pl.ds(r0, tg), :].astype(F32)
        upad_ref[pl.ds(pad + r0, tg), :] = a * _sigmoid(b)
        return carry

    lax.fori_loop(0, l // tg, glu_body, 0)

    tile = 64
    halo = 2 * pad
    lng = lng_ref[...]
    lnb = lnb_ref[...]

    def conv_body(t, carry):
        r0 = pl.multiple_of(t * tile, tile)
        for cb in range(CONV_W // LANES):
            cols = slice(cb * LANES, (cb + 1) * LANES)
            win = upad_ref[pl.ds(r0, tile + halo), cols]
            acc = jnp.zeros((tile, LANES), F32) + dwb_ref[:, cols]
            for r in range(SUBLANES):
                shifted = win[r:r + tile + halo - SUBLANES]
                for a in range(halo // SUBLANES):
                    k = SUBLANES * a + r - 1
                    if 0 <= k < CONV_K:
                        acc = acc + shifted[SUBLANES * a:SUBLANES * a + tile] * dww_ref[k:k + 1, cols]
            y_ref[:, cols] = acc
        y = y_ref[...]
        mu = jnp.mean(y, axis=-1, keepdims=True)
        d = y - mu
        var = jnp.mean(d * d, axis=-1, keepdims=True)
        u = d * lax.rsqrt(var + EPS) * lng + lnb
        act_ref[pl.ds(r0, tile), :] = (u * _sigmoid(u)).astype(BF16)
        return carry

    lax.fori_loop(0, l // tile, conv_body, 0)

    tmm = min(l, 512)

    def mm_body(t, carry):
        r0 = pl.multiple_of(t * tmm, tmm)
        o_ref[0, pl.ds(r0, tmm), :] = jnp.dot(act_ref[pl.ds(r0, tmm), :], pw_ref[...],
                                              preferred_element_type=F32).astype(BF16)
        return carry

    lax.fori_loop(0, l // tmm, mm_body, 0)


def _conv_module(p, dw_w, dw_b, ln_g, ln_b, pw):
    b, l, _ = p.shape
    vec = pl.BlockSpec((1, CONV_W), lambda b: (0, 0))
    return pl.pallas_call(
        functools.partial(_conv_kernel, l=l),
        grid=(b,),
        in_specs=[
            pl.BlockSpec((1, l, CONV_W), lambda b: (b, 0, OFF_CA // CONV_W)),
            pl.BlockSpec((1, l, CONV_W), lambda b: (b, 0, OFF_CB // CONV_W)),
            pl.BlockSpec((CONV_K, CONV_W), lambda b: (0, 0)),
            vec, vec, vec,
            pl.BlockSpec((CONV_W, CONV_W), lambda b: (0, 0)),
        ],
        out_specs=pl.BlockSpec((1, l, CONV_W), lambda b: (b, 0, 0)),
        out_shape=jax.ShapeDtypeStruct((b, l, CONV_W), BF16),
        scratch_shapes=[pltpu.VMEM((l + 2 * CONV_PAD, CONV_W), F32), pltpu.VMEM((64, CONV_W), F32),
                        pltpu.VMEM((l, CONV_W), BF16)],
        compiler_params=_params("parallel"),
        name="conv_module",
    )(p, p, dw_w, dw_b.reshape(1, CONV_W), ln_g.reshape(1, CONV_W), ln_b.reshape(1, CONV_W), pw)


NA_RPB_ROWS = 2 * NA_ROWS - 1
NA_RPB_COLS = 2 * NA_COLS - 1
NA_PAIRS = NA_RPB_ROWS - 1


def _na_kernel(rpb_ref, q_ref, k_ref, v_ref, kc_ref, vc_ref, o_ref, pairs_ref, *, rows_n, kh):
    scale = NA_DH ** -0.5
    h = pl.program_id(0)
    nt = (((1,), (1,)), ((), ()))

    @pl.when(pl.program_id(1) == 0)
    def _build_bias():
        lane = lax.broadcasted_iota(jnp.int32, (GRID_W, 2 * GRID_W), 1)
        qcol = lax.broadcasted_iota(jnp.int32, (GRID_W, 2 * GRID_W), 0)
        w = lane & (GRID_W - 1)
        col_off = jnp.clip(w - qcol + (NA_COLS - 1), 0, NA_RPB_COLS - 1)
        col_start = jnp.clip(qcol - NA_COLS // 2, 0, GRID_W - NA_COLS)
        col_in = (w >= col_start) & (w < col_start + NA_COLS)
        left = lane < GRID_W

        def pair_body(k, carry):
            row = h * NA_RPB_ROWS + k
            acc = jnp.zeros((GRID_W, 2 * GRID_W), F32)
            for d in range(NA_RPB_COLS):
                val = jnp.where(left, rpb_ref[row, d], rpb_ref[row + 1, d])
                acc = jnp.where(col_off == d, val, acc)
            pairs_ref[k] = jnp.where(col_in, acc, MASK_VALUE)
            return carry

        lax.fori_loop(0, NA_PAIRS, pair_body, 0)

    kc = kc_ref[0]
    vc = vc_ref[0]

    def body(r, carry):
        s0 = jnp.clip(r - kh // 2, 0, rows_n - kh)
        sidx = s0 - r + (NA_ROWS - 1)
        q0 = pl.multiple_of(r * GRID_W, GRID_W)
        k0 = pl.multiple_of(s0 * GRID_W, GRID_W)
        q = q_ref[0, pl.ds(q0, GRID_W), :]
        kw = k_ref[0, pl.ds(k0, kh * GRID_W), :]
        vw = v_ref[0, pl.ds(k0, kh * GRID_W), :]
        bias = jnp.concatenate([pairs_ref[sidx + 2 * m] for m in range(kh // 2)], axis=-1)
        s_lat = lax.dot_general(q, kw, nt, preferred_element_type=F32) * scale + bias
        s_ctx = lax.dot_general(q, kc, nt, preferred_element_type=F32) * scale
        m = jnp.maximum(jnp.max(s_lat, axis=-1, keepdims=True), jnp.max(s_ctx, axis=-1, keepdims=True))
        e_lat = jnp.exp(s_lat - m)
        e_ctx = jnp.exp(s_ctx - m)
        den = jnp.sum(e_lat, axis=-1, keepdims=True) + jnp.sum(e_ctx, axis=-1, keepdims=True)
        o = (jnp.dot(e_lat.astype(BF16), vw, preferred_element_type=F32)
             + jnp.dot(e_ctx.astype(BF16), vc, preferred_element_type=F32))
        o_ref[0, pl.ds(q0, GRID_W), :] = (o / den).astype(BF16)
        return carry

    lax.fori_loop(0, rows_n, body, 0)


def _na_latent(p_l, p_c, rpb):
    b, l, _ = p_l.shape
    lc = p_c.shape[1]
    rows_n = l // GRID_W
    kh = NA_ROWS
    assert rows_n >= NA_ROWS and rpb.shape == (NA_HEADS, NA_RPB_ROWS, NA_RPB_COLS)

    def col(off, length):
        return pl.BlockSpec((1, length, NA_DH), lambda h, b: (b, 0, off // NA_DH + h))

    return pl.pallas_call(
        functools.partial(_na_kernel, rows_n=rows_n, kh=kh),
        grid=(NA_HEADS, b),
        in_specs=[pl.BlockSpec(memory_space=pltpu.SMEM),
                  col(OFF_NQ, l), col(OFF_NK, l), col(OFF_NV, l), col(OFF_NK, lc), col(OFF_NV, lc)],
        out_specs=pl.BlockSpec((1, l, NA_DH), lambda h, b: (b, 0, h)),
        out_shape=jax.ShapeDtypeStruct((b, l, NA_W), BF16),
        scratch_shapes=[pltpu.VMEM((NA_PAIRS, GRID_W, 2 * GRID_W), F32)],
        compiler_params=_params("parallel", "arbitrary"),
        name="na_latent",
    )(rpb.astype(F32).reshape(NA_HEADS * NA_RPB_ROWS, NA_RPB_COLS), p_l, p_l, p_l, p_c, p_c)


def _na_ctx_kernel(q_ref, k_ref, v_ref, o_ref):
    scale = NA_DH ** -0.5
    s = lax.dot_general(q_ref[0], k_ref[0], (((1,), (1,)), ((), ())), preferred_element_type=F32) * scale
    e = jnp.exp(s - jnp.max(s, axis=-1, keepdims=True))
    o = jnp.dot(e.astype(BF16), v_ref[0], preferred_element_type=F32)
    o_ref[0] = (o / jnp.sum(e, axis=-1, keepdims=True)).astype(BF16)


def _na_context(p_c):
    b, lc, _ = p_c.shape

    def col(off):
        return pl.BlockSpec((1, lc, NA_DH), lambda b, h: (b, 0, off // NA_DH + h))

    return pl.pallas_call(
        _na_ctx_kernel,
        grid=(b, NA_HEADS),
        in_specs=[col(OFF_NQ), col(OFF_NK), col(OFF_NV)],
        out_specs=pl.BlockSpec((1, lc, NA_DH), lambda b, h: (b, 0, h)),
        out_shape=jax.ShapeDtypeStruct((b, lc, NA_W), BF16),
        compiler_params=_params("parallel", "parallel"),
        name="na_context",
    )(p_c, p_c, p_c)


def _ffn_act_kernel(val_ref, gate_ref, vprev_ref, gprev_ref, vnext_ref, gnext_ref, wv_ref, wg_ref, bv_ref, bg_ref,
                    o_ref, *, halo_rows):
    i = pl.program_id(1)
    last = pl.num_programs(1) - 1
    t = val_ref.shape[1]
    row = lax.broadcasted_iota(jnp.int32, (t, 1), 0)

    def conv3(x_ref, prev_ref, next_ref, w_ref, b_ref):
        x = x_ref[0].astype(F32)
        prev_row = jnp.where(i == 0, 0.0, prev_ref[0, halo_rows - 1:halo_rows, :].astype(F32))
        next_row = jnp.where(i == last, 0.0, next_ref[0, 0:1, :].astype(F32))
        x_prev = jnp.where(row == 0, prev_row, pltpu.roll(x, 1, 0))
        x_next = jnp.where(row == t - 1, next_row, pltpu.roll(x, t - 1, 0))
        return x_prev * w_ref[0:1, :] + x * w_ref[1:2, :] + x_next * w_ref[2:3, :] + b_ref[...]

    val = conv3(val_ref, vprev_ref, vnext_ref, wv_ref, bv_ref)
    gate = conv3(gate_ref, gprev_ref, gnext_ref, wg_ref, bg_ref)
    o_ref[0] = (gate * _sigmoid(gate) * val).astype(BF16)


def _ffn_act(u, dw_w, dw_b):
    b, l, f2 = u.shape
    f = f2 // 2
    t = min(l, 512)
    tc = 512
    nc = f // tc
    hr = 16
    nrb = l // hr

    def main(off):
        return pl.BlockSpec((1, t, tc), lambda b, i, j: (b, i, off + j))

    def prev(off):
        return pl.BlockSpec((1, hr, tc), lambda b, i, j: (b, jnp.maximum(i * (t // hr) - 1, 0), off + j))

    def nxt(off):
        return pl.BlockSpec((1, hr, tc), lambda b, i, j: (b, jnp.minimum((i + 1) * (t // hr), nrb - 1), off + j))

    def wspec(rows, off):
        return pl.BlockSpec((rows, tc), lambda b, i, j: (0, off + j))

    return pl.pallas_call(
        functools.partial(_ffn_act_kernel, halo_rows=hr),
        grid=(b, l // t, nc),
        in_specs=[main(0), main(nc), prev(0), prev(nc), nxt(0), nxt(nc),
                  wspec(FFN_K, 0), wspec(FFN_K, nc), wspec(1, 0), wspec(1, nc)],
        out_specs=pl.BlockSpec((1, t, tc), lambda b, i, j: (b, i, j)),
        out_shape=jax.ShapeDtypeStruct((b, l, f), BF16),
        compiler_params=_params("parallel", "parallel", "parallel"),
        name="ffn_act",
    )(u, u, u, u, u, u, dw_w, dw_w, dw_b.reshape(1, f2), dw_b.reshape(1, f2))


def _rmsnorm_kernel(x_ref, g_ref, o_ref):
    x = x_ref[0]
    ms = jnp.mean(x * x, axis=-1, keepdims=True)
    o_ref[0] = x * lax.rsqrt(ms + EPS) * g_ref[...]


def _rmsnorm(x, g):
    b, l, d = x.shape
    tm = min(l, 512)
    return pl.pallas_call(
        _rmsnorm_kernel,
        grid=(b, l // tm),
        in_specs=[pl.BlockSpec((1, tm, d), lambda b, i: (b, i, 0)), pl.BlockSpec((1, d), lambda b, i: (0, 0))],
        out_specs=pl.BlockSpec((1, tm, d), lambda b, i: (b, i, 0)),
        out_shape=jax.ShapeDtypeStruct((b, l, d), F32),
        compiler_params=_params("parallel", "parallel"),
        name="final_rmsnorm",
    )(x, g.reshape(1, d))


def _rope_tables(l):
    nf = RET_DK // 4
    pos = np.arange(l)
    inv = ROPE_BASE ** (-jnp.arange(nf, dtype=F32) / nf)
    ang_r = jnp.asarray(pos // GRID_W, F32)[:, None] * inv[None, :]
    ang_c = jnp.asarray(pos % GRID_W, F32)[:, None] * inv[None, :]
    cos = jnp.concatenate([jnp.cos(ang_r), jnp.cos(ang_r), jnp.cos(ang_c), jnp.cos(ang_c)], axis=-1)
    sin = jnp.concatenate([-jnp.sin(ang_r), jnp.sin(ang_r), -jnp.sin(ang_c), jnp.sin(ang_c)], axis=-1)
    return cos, sin


def kernel(x, c, ctx, c_ctx, w_ada, b_ada, norm1_g, w_in, ret_decay, ret_gn_g, conv_dw_w, conv_dw_b, conv_ln_g,
           conv_ln_b, conv_pw, na_rpb, w_out, norm2_g, ffn_up, ffn_dw_w, ffn_dw_b, ffn_down, final_g):
    depth = w_ada.shape[0]
    bsz, l, d = x.shape

    cond = jnp.zeros((COND_ROWS, d), F32).at[:bsz].set(c).at[bsz].set(c_ctx)
    mod = _adaln(cond, w_ada, b_ada).reshape(depth, COND_ROWS, 1, 6 * d)
    lat_row = lambda b: b
    ctx_row = lambda b: bsz
    cos, sin = _rope_tables(l)
    log_gamma = jax.nn.log_sigmoid(ret_decay.astype(F32))
    w_in_b = w_in.astype(BF16)
    w_out_b = w_out.astype(BF16)
    up_b = ffn_up.astype(BF16)
    down_b = ffn_down.astype(BF16)
    tn_in = w_in.shape[-1] // 4
    tn_up = ffn_up.shape[-1] // 8

    h_ctx = ctx
    for layer in range(depth):
        last = layer == depth - 1
        pw_b = conv_pw[layer].astype(BF16)

        p_l = _norm_mod_matmul(x, norm1_g[layer], mod, layer, lat_row, 0, 1, w_in_b, tn_in, "in_proj")
        p_c = _norm_mod_matmul(h_ctx, norm1_g[layer], mod, layer, ctx_row, 0, 1, w_in_b, tn_in, "in_proj_ctx")

        ret_l, ret_c = _retention(p_l, p_c, log_gamma[layer], ret_gn_g[layer], cos, sin, with_ctx=not last)
        conv_l = _conv_module(p_l, conv_dw_w[layer], conv_dw_b[layer], conv_ln_g[layer], conv_ln_b[layer], pw_b)
        na_l = _na_latent(p_l, p_c, na_rpb[layer])
        x = _matmul_residual([ret_l, conv_l, na_l], w_out_b, x, mod, layer, lat_row, 2, 1024, "out_proj")

        u = _norm_mod_matmul(x, norm2_g[layer], mod, layer, lat_row, 3, 4, up_b, tn_up, "ffn_up")
        act = _ffn_act(u, ffn_dw_w[layer], ffn_dw_b[layer])
        x = _matmul_residual([act], down_b, x, mod, layer, lat_row, 5, 512, "ffn_down")

        if not last:
            conv_c = _conv_module(p_c, conv_dw_w[layer], conv_dw_b[layer], conv_ln_g[layer], conv_ln_b[layer], pw_b)
            na_c = _na_context(p_c)
            h_ctx = _matmul_residual([ret_c, conv_c, na_c], w_out_b, h_ctx, mod, layer, ctx_row, 2, 1024,
                                     "out_proj_ctx")
            u_c = _norm_mod_matmul(h_ctx, norm2_g[layer], mod, layer, ctx_row, 3, 4, up_b, tn_up, "ffn_up_ctx")
            act_c = _ffn_act(u_c, ffn_dw_w[layer], ffn_dw_b[layer])
            h_ctx = _matmul_residual([act_c], down_b, h_ctx, mod, layer, ctx_row, 5, 512, "ffn_down_ctx")

    return _rmsnorm(x, final_g)
```

```python
import functools

import numpy as np
import jax
import jax.numpy as jnp
from jax import lax
from jax.experimental import pallas as pl
from jax.experimental.pallas import tpu as pltpu

F32 = jnp.float32
BF16 = jnp.bfloat16

GRID_W = 64
RET_HEADS = 4
RET_DK = 128
RET_DV = 256
RET_CHUNK = 128
RET_UNROLL = 8
CONV_W = 512
CONV_K = 31
NA_HEADS = 4
NA_DH = 128
NA_ROWS = 8
NA_COLS = 16
FFN_K = 3
ROPE_BASE = 10000.0
EPS = 1e-6

RET_QK_W = RET_HEADS * RET_DK
RET_W = RET_HEADS * RET_DV
NA_W = NA_HEADS * NA_DH
OFF_RQ = 0
OFF_RK = OFF_RQ + RET_QK_W
OFF_RV = OFF_RK + RET_QK_W
OFF_RG = OFF_RV + RET_W
OFF_CA = OFF_RG + RET_W
OFF_CB = OFF_CA + CONV_W
OFF_NQ = OFF_CB + CONV_W
OFF_NK = OFF_NQ + NA_W
OFF_NV = OFF_NK + NA_W

V7X_VMEM_BYTES = 64 * 1024 * 1024
VMEM_LIMIT = V7X_VMEM_BYTES - 8 * 1024 * 1024
SUBLANES = 8
LANES = 128
COND_ROWS = 8
MASK_VALUE = -1e30
CONV_PAD = 16
CONV_TILE = 64


def _sigmoid(x):
    return 1.0 / (1.0 + jnp.exp(-x))


def _params(*sem):
    return pltpu.CompilerParams(dimension_semantics=sem, vmem_limit_bytes=VMEM_LIMIT)


def _adaln_kernel(c_ref, w_ref, b_ref, o_ref):
    c = c_ref[...]
    s = (c * _sigmoid(c)).astype(BF16)
    o_ref[0] = jnp.dot(s, w_ref[0].astype(BF16), preferred_element_type=F32) + b_ref[0]


def _adaln(cond, w_ada, b_ada):
    depth, d, n = w_ada.shape
    tn = 1024
    return pl.pallas_call(
        _adaln_kernel,
        grid=(depth, n // tn),
        in_specs=[
            pl.BlockSpec((COND_ROWS, d), lambda l, j: (0, 0)),
            pl.BlockSpec((1, d, tn), lambda l, j: (l, 0, j)),
            pl.BlockSpec((1, 1, tn), lambda l, j: (l, 0, j)),
        ],
        out_specs=pl.BlockSpec((1, COND_ROWS, tn), lambda l, j: (l, 0, j)),
        out_shape=jax.ShapeDtypeStruct((depth, COND_ROWS, n), F32),
        compiler_params=_params("parallel", "parallel"),
        name="adaln",
    )(cond, w_ada, b_ada.reshape(depth, 1, n))


def _mod_spec(layer, d, chunk, row):
    return pl.BlockSpec((None, 1, 1, d), lambda b, i, j: (layer, row(b), 0, chunk))


def _norm_mod_matmul_kernel(x_ref, g_ref, sh_ref, sc_ref, w_ref, o_ref, h_ref):
    @pl.when(pl.program_id(2) == 0)
    def _():
        x = x_ref[0]
        ms = jnp.mean(x * x, axis=-1, keepdims=True)
        y = x * lax.rsqrt(ms + EPS) * g_ref[...]
        h_ref[...] = (y * (1.0 + sc_ref[0]) + sh_ref[0]).astype(BF16)

    o_ref[0] = jnp.dot(h_ref[...], w_ref[...], preferred_element_type=F32).astype(o_ref.dtype)


def _norm_mod_matmul(x, g, mod, layer, row, shift_chunk, scale_chunk, w, tn, name):
    b, l, d = x.shape
    n = w.shape[-1]
    tm = min(l, 1024)
    return pl.pallas_call(
        _norm_mod_matmul_kernel,
        grid=(b, l // tm, n // tn),
        in_specs=[
            pl.BlockSpec((1, tm, d), lambda b, i, j: (b, i, 0)),
            pl.BlockSpec((1, d), lambda b, i, j: (0, 0)),
            _mod_spec(layer, d, shift_chunk, row),
            _mod_spec(layer, d, scale_chunk, row),
            pl.BlockSpec((None, d, tn), lambda b, i, j: (layer, 0, j)),
        ],
        out_specs=pl.BlockSpec((1, tm, tn), lambda b, i, j: (b, i, j)),
        out_shape=jax.ShapeDtypeStruct((b, l, n), BF16),
        scratch_shapes=[pltpu.VMEM((tm, d), BF16)],
        compiler_params=_params("parallel", "parallel", "arbitrary"),
        name=name,
    )(x, g.reshape(1, d), mod, mod, w)


def _matmul_residual_kernel(*refs, n_a):
    a_refs = refs[:n_a]
    w_ref, res_ref, gate_ref, o_ref = refs[n_a:]
    acc = None
    off = 0
    for a_ref in a_refs:
        k = a_ref.shape[-1]
        part = jnp.dot(a_ref[0], w_ref[off:off + k, :], preferred_element_type=F32)
        acc = part if acc is None else acc + part
        off += k
    o_ref[0] = res_ref[0] + gate_ref[0] * acc


def _matmul_residual(a_list, w, res, mod, layer, row, gate_chunk, tn, name):
    b, l, n = res.shape
    k_total = w.shape[1]
    tm = min(l, 1024)
    tn = min(tn, n)
    in_specs = [pl.BlockSpec((1, tm, a.shape[-1]), lambda b, i, j: (b, i, 0)) for a in a_list]
    in_specs += [
        pl.BlockSpec((None, k_total, tn), lambda b, i, j: (layer, 0, j)),
        pl.BlockSpec((1, tm, tn), lambda b, i, j: (b, i, j)),
        pl.BlockSpec((None, 1, 1, tn), lambda b, i, j: (layer, row(b), 0, gate_chunk * (n // tn) + j)),
    ]
    return pl.pallas_call(
        functools.partial(_matmul_residual_kernel, n_a=len(a_list)),
        grid=(b, l // tm, n // tn),
        in_specs=in_specs,
        out_specs=pl.BlockSpec((1, tm, tn), lambda b, i, j: (b, i, j)),
        out_shape=jax.ShapeDtypeStruct((b, l, n), F32),
        compiler_params=_params("parallel", "parallel", "arbitrary"),
        name=name,
    )(*a_list, w, res, mod)


def _rope(t, cos, sin):
    lane = lax.broadcasted_iota(jnp.int32, t.shape, 1)
    partner = jnp.where((lane & 32) == 0, pltpu.roll(t, LANES - 32, 1), pltpu.roll(t, 32, 1))
    return t * cos + partner * sin


def _retention_kernel(lg_ref, ql_ref, kl_ref, vl_ref, gl_ref, qc_ref, kc_ref, vc_ref, gc_ref, cos_ref, sin_ref,
                      gn_ref, *rest, with_ctx, n_lat, n_ctx):
    if with_ctx:
        ol_ref, oc_ref, acc_ref, qr_ref, kr_ref, accc_ref = rest
    else:
        ol_ref, acc_ref, qr_ref, kr_ref = rest
    c = RET_CHUNK
    scale = RET_DK ** -0.5
    h = pl.program_id(1)
    lgf = lg_ref[0, h]
    lgb = lg_ref[1, h]
    ii = lax.broadcasted_iota(jnp.int32, (c, c), 0).astype(F32)
    jj = lax.broadcasted_iota(jnp.int32, (c, c), 1).astype(F32)
    diff = ii - jj
    decay = (jnp.where(diff >= 0, jnp.exp(jnp.maximum(diff, 0.0) * lgf), 0.0)
             + jnp.where(diff <= 0, jnp.exp(jnp.maximum(-diff, 0.0) * lgb), 0.0))
    ic = lax.broadcasted_iota(jnp.int32, (c, 1), 0).astype(F32)
    xi_f = jnp.exp((ic + 1.0) * lgf)
    zeta_f = jnp.exp((c - 1.0 - ic) * lgf)
    xi_b = jnp.exp((c - ic) * lgb)
    zeta_b = jnp.exp(ic * lgb)
    g_f = jnp.exp(jnp.full((1, RET_DV), float(c), F32) * lgf)
    g_b = jnp.exp(jnp.full((1, RET_DV), float(c), F32) * lgb)
    gn = gn_ref[...]

    def intra_and_inter(qb, kb, v, xi, st):
        s = lax.dot_general(qb, kb, (((1,), (1,)), ((), ())), preferred_element_type=F32)
        p = (s * decay).astype(BF16)
        inter = jnp.dot(qb, st.astype(BF16), preferred_element_type=F32) * xi
        return jnp.dot(p, v, preferred_element_type=F32) + inter

    def next_state(st, k_f32, v, zeta, g):
        kz = (k_f32 * zeta).T.astype(BF16)
        return st * g + jnp.dot(kz, v, preferred_element_type=F32)

    def finish(total, gate):
        mu = jnp.mean(total, axis=-1, keepdims=True)
        d = total - mu
        var = jnp.mean(d * d, axis=-1, keepdims=True)
        y = d * lax.rsqrt(var + EPS) * gn
        return (y * (gate * _sigmoid(gate))).astype(BF16)

    st = jnp.zeros((RET_DK, RET_DV), F32)
    for j in range(n_ctx):
        rows = slice(j * c, (j + 1) * c)
        q = qc_ref[0, rows, :].astype(F32) * scale
        k = kc_ref[0, rows, :].astype(F32)
        v = vc_ref[0, rows, :]
        if with_ctx:
            accc_ref[rows, :] = intra_and_inter(q.astype(BF16), k.astype(BF16), v, xi_f, st)
        st = next_state(st, k, v, zeta_f, g_f)

    nt = (((1,), (1,)), ((), ()))
    group = RET_UNROLL
    assert n_lat % group == 0

    def fwd_body(jg, st):
        rows, qb, kb, v, kzv = [], [], [], [], []
        for u in range(group):
            r = pl.ds(pl.multiple_of((jg * group + u) * c, c), c)
            cos = cos_ref[r, :]
            sin = sin_ref[r, :]
            q = _rope(ql_ref[0, r, :].astype(F32), cos, sin) * scale
            k = _rope(kl_ref[0, r, :].astype(F32), cos, sin)
            rows.append(r)
            qb.append(q.astype(BF16))
            kb.append(k.astype(BF16))
            v.append(vl_ref[0, r, :])
            qr_ref[r, :] = qb[u]
            kr_ref[r, :] = kb[u]
            kzv.append(jnp.dot((k * zeta_f).T.astype(BF16), v[u], preferred_element_type=F32))
        s = [lax.dot_general(qb[u], kb[u], nt, preferred_element_type=F32) for u in range(group)]
        states = []
        for u in range(group):
            states.append(st.astype(BF16))
            st = st * g_f + kzv[u]
        p = [(s[u] * decay).astype(BF16) for u in range(group)]
        for u in range(group):
            inter = jnp.dot(qb[u], states[u], preferred_element_type=F32) * xi_f
            acc_ref[rows[u], :] = jnp.dot(p[u], v[u], preferred_element_type=F32) + inter
        return st

    lax.fori_loop(0, n_lat // group, fwd_body, st)

    st = jnp.zeros((RET_DK, RET_DV), F32)
    for j in reversed(range(n_ctx)):
        rows = slice(j * c, (j + 1) * c)
        k = kc_ref[0, rows, :].astype(F32)
        v = vc_ref[0, rows, :]
        if with_ctx:
            qb = (qc_ref[0, rows, :].astype(F32) * scale).astype(BF16)
            inter = jnp.dot(qb, st.astype(BF16), preferred_element_type=F32) * xi_b
            oc_ref[0, rows, :] = finish(accc_ref[rows, :] + inter, gc_ref[0, rows, :].astype(F32))
        st = next_state(st, k, v, zeta_b, g_b)

    def bwd_body(jg, st):
        rows, kzv = [], []
        for u in range(group):
            r = pl.ds(pl.multiple_of((n_lat - 1 - (jg * group + u)) * c, c), c)
            rows.append(r)
            kz = (kr_ref[r, :].astype(F32) * zeta_b).T.astype(BF16)
            kzv.append(jnp.dot(kz, vl_ref[0, r, :], preferred_element_type=F32))
        states = []
        for u in range(group):
            states.append(st.astype(BF16))
            st = st * g_b + kzv[u]
        inter = [jnp.dot(qr_ref[rows[u], :], states[u], preferred_element_type=F32) * xi_b for u in range(group)]
        for u in range(group):
            ol_ref[0, rows[u], :] = finish(acc_ref[rows[u], :] + inter[u], gl_ref[0, rows[u], :].astype(F32))
        return st

    lax.fori_loop(0, n_lat // group, bwd_body, st)


def _retention(p_l, p_c, log_gamma, gn_g, cos, sin, with_ctx):
    b, l, _ = p_l.shape
    lc = p_c.shape[1]
    c = RET_CHUNK

    def col(width, off, length):
        return pl.BlockSpec((1, length, width), lambda b, h: (b, 0, off // width + h))

    in_specs = [
        pl.BlockSpec(memory_space=pltpu.SMEM),
        col(RET_DK, OFF_RQ, l), col(RET_DK, OFF_RK, l), col(RET_DV, OFF_RV, l), col(RET_DV, OFF_RG, l),
        col(RET_DK, OFF_RQ, lc), col(RET_DK, OFF_RK, lc), col(RET_DV, OFF_RV, lc), col(RET_DV, OFF_RG, lc),
        pl.BlockSpec((l, RET_DK), lambda b, h: (0, 0)),
        pl.BlockSpec((l, RET_DK), lambda b, h: (0, 0)),
        pl.BlockSpec((1, RET_DV), lambda b, h: (0, h)),
    ]
    out_specs = [pl.BlockSpec((1, l, RET_DV), lambda b, h: (b, 0, h))]
    out_shape = [jax.ShapeDtypeStruct((b, l, RET_W), BF16)]
    scratch = [pltpu.VMEM((l, RET_DV), F32), pltpu.VMEM((l, RET_DK), BF16), pltpu.VMEM((l, RET_DK), BF16)]
    if with_ctx:
        out_specs.append(pl.BlockSpec((1, lc, RET_DV), lambda b, h: (b, 0, h)))
        out_shape.append(jax.ShapeDtypeStruct((b, lc, RET_W), BF16))
        scratch.append(pltpu.VMEM((lc, RET_DV), F32))
    outs = pl.pallas_call(
        functools.partial(_retention_kernel, with_ctx=with_ctx, n_lat=l // c, n_ctx=lc // c),
        grid=(b, RET_HEADS),
        in_specs=in_specs,
        out_specs=out_specs,
        out_shape=out_shape,
        scratch_shapes=scratch,
        compiler_params=_params("parallel", "parallel"),
        name="retention_ctx" if with_ctx else "retention",
    )(log_gamma, p_l, p_l, p_l, p_l, p_c, p_c, p_c, p_c, cos, sin, gn_g.reshape(1, RET_W))
    return (outs[0], outs[1]) if with_ctx else (outs[0], None)


def _conv_kernel(a_ref, b_ref, dww_ref, dwb_ref, lng_ref, lnb_ref, pw_ref, o_ref, upad_ref, act_ref, *, l):
    pad = CONV_PAD
    upad_ref[0:pad, :] = jnp.zeros((pad, CONV_W), F32)
    upad_ref[pad + l:pad + l + pad, :] = jnp.zeros((pad, CONV_W), F32)
    tg = min(l, 256)

    def glu_body(t, carry):
        r0 = pl.multiple_of(t * tg, tg)
        a = a_ref[0, pl.ds(r0, tg), :].astype(F32)
        b = b_ref[0, pl.ds(r0, tg), :].astype(F32)
        upad_ref[pl.ds(pad + r0, tg), :] = a * _sigmoid(b)
        return carry

    lax.fori_loop(0, l // tg, glu_body, 0)

    tile = CONV_TILE
    halo = 2 * pad
    lng = lng_ref[...]
    lnb = lnb_ref[...]

    def conv_body(t, carry):
        r0 = pl.multiple_of(t * tile, tile)
        ys = []
        for cb in range(CONV_W // LANES):
            cols = slice(cb * LANES, (cb + 1) * LANES)
            win = upad_ref[pl.ds(r0, tile + halo), cols]
            acc = jnp.zeros((tile, LANES), F32) + dwb_ref[:, cols]
            for r in range(SUBLANES):
                rolled = win if r == 0 else pltpu.roll(win, tile + halo - r, 0)
                for a in range(halo // SUBLANES):
                    k = SUBLANES * a + r - 1
                    if 0 <= k < CONV_K:
                        acc = acc + rolled[SUBLANES * a:SUBLANES * a + tile] * dww_ref[k:k + 1, cols]
            ys.append(acc)
        y = jnp.concatenate(ys, axis=-1)
        mu = jnp.mean(y, axis=-1, keepdims=True)
        d = y - mu
        var = jnp.mean(d * d, axis=-1, keepdims=True)
        u = d * lax.rsqrt(var + EPS) * lng + lnb
        act_ref[pl.ds(r0, tile), :] = (u * _sigmoid(u)).astype(BF16)
        return carry

    lax.fori_loop(0, l // tile, conv_body, 0)

    tmm = min(l, 512)

    def mm_body(t, carry):
        r0 = pl.multiple_of(t * tmm, tmm)
        o_ref[0, pl.ds(r0, tmm), :] = jnp.dot(act_ref[pl.ds(r0, tmm), :], pw_ref[...],
                                              preferred_element_type=F32).astype(BF16)
        return carry

    lax.fori_loop(0, l // tmm, mm_body, 0)


def _conv_module(p, dw_w, dw_b, ln_g, ln_b, pw):
    b, l, _ = p.shape
    vec = pl.BlockSpec((1, CONV_W), lambda b: (0, 0))
    return pl.pallas_call(
        functools.partial(_conv_kernel, l=l),
        grid=(b,),
        in_specs=[
            pl.BlockSpec((1, l, CONV_W), lambda b: (b, 0, OFF_CA // CONV_W)),
            pl.BlockSpec((1, l, CONV_W), lambda b: (b, 0, OFF_CB // CONV_W)),
            pl.BlockSpec((CONV_K, CONV_W), lambda b: (0, 0)),
            vec, vec, vec,
            pl.BlockSpec((CONV_W, CONV_W), lambda b: (0, 0)),
        ],
        out_specs=pl.BlockSpec((1, l, CONV_W), lambda b: (b, 0, 0)),
        out_shape=jax.ShapeDtypeStruct((b, l, CONV_W), BF16),
        scratch_shapes=[pltpu.VMEM((l + 2 * CONV_PAD, CONV_W), F32),
                        pltpu.VMEM((l, CONV_W), BF16)],
        compiler_params=_params("parallel"),
        name="conv_module",
    )(p, p, dw_w, dw_b.reshape(1, CONV_W), ln_g.reshape(1, CONV_W), ln_b.reshape(1, CONV_W), pw)


NA_RPB_ROWS = 2 * NA_ROWS - 1
NA_RPB_COLS = 2 * NA_COLS - 1
NA_PAIRS = NA_RPB_ROWS - 1
NA_QROWS = 4
NA_UNION = NA_ROWS + NA_QROWS


def _na_kernel(rpb_ref, q_ref, k_ref, v_ref, kc_ref, vc_ref, o_ref, pairs_ref, *, rows_n, kh):
    scale = NA_DH ** -0.5
    h = pl.program_id(0)
    nt = (((1,), (1,)), ((), ()))

    @pl.when(pl.program_id(1) == 0)
    def _build_bias():
        lane = lax.broadcasted_iota(jnp.int32, (GRID_W, 2 * GRID_W), 1)
        qcol = lax.broadcasted_iota(jnp.int32, (GRID_W, 2 * GRID_W), 0)
        w = lane & (GRID_W - 1)
        col_off = jnp.clip(w - qcol + (NA_COLS - 1), 0, NA_RPB_COLS - 1)
        col_start = jnp.clip(qcol - NA_COLS // 2, 0, GRID_W - NA_COLS)
        col_in = (w >= col_start) & (w < col_start + NA_COLS)
        left = lane < GRID_W

        def pair_body(k, carry):
            row = h * NA_RPB_ROWS + k
            acc = jnp.zeros((GRID_W, 2 * GRID_W), F32)
            for d in range(NA_RPB_COLS):
                val = jnp.where(left, rpb_ref[row, d], rpb_ref[row + 1, d])
                acc = jnp.where(col_off == d, val, acc)
            pairs_ref[k] = jnp.where(col_in, acc, MASK_VALUE)
            return carry

        lax.fori_loop(0, NA_PAIRS, pair_body, 0)

    kc = kc_ref[0]
    vc = vc_ref[0]

    lane = lax.broadcasted_iota(jnp.int32, (GRID_W, 2 * GRID_W), 1)
    left = lane < GRID_W
    nq = NA_QROWS * GRID_W
    nk = NA_UNION * GRID_W

    def body(g, carry):
        r0 = g * NA_QROWS
        u0 = jnp.clip(r0 - kh // 2, 0, rows_n - NA_UNION)
        q0 = pl.multiple_of(r0 * GRID_W, nq)
        k0 = pl.multiple_of(u0 * GRID_W, GRID_W)
        q = q_ref[0, pl.ds(q0, nq), :]
        kw = k_ref[0, pl.ds(k0, nk), :]
        vw = v_ref[0, pl.ds(k0, nk), :]
        bias_rows = []
        for rq in range(NA_QROWS):
            r = r0 + rq
            s0 = jnp.clip(r - kh // 2, 0, rows_n - kh)
            tiles = []
            for m in range(NA_UNION // 2):
                kr = u0 + 2 * m
                ok_l = (kr >= s0) & (kr < s0 + kh)
                ok_r = (kr + 1 >= s0) & (kr + 1 < s0 + kh)
                ok = jnp.where(left, ok_l.astype(jnp.int32), ok_r.astype(jnp.int32)) != 0
                pair = pairs_ref[jnp.clip(kr - r + (NA_ROWS - 1), 0, NA_PAIRS - 1)]
                tiles.append(jnp.where(ok, pair, MASK_VALUE))
            bias_rows.append(jnp.concatenate(tiles, axis=-1))
        bias = jnp.concatenate(bias_rows, axis=0)
        s_lat = lax.dot_general(q, kw, nt, preferred_element_type=F32) * scale + bias
        s_ctx = lax.dot_general(q, kc, nt, preferred_element_type=F32) * scale
        m = jnp.maximum(jnp.max(s_lat, axis=-1, keepdims=True), jnp.max(s_ctx, axis=-1, keepdims=True))
        e_lat = jnp.exp(s_lat - m)
        e_ctx = jnp.exp(s_ctx - m)
        den = jnp.sum(e_lat, axis=-1, keepdims=True) + jnp.sum(e_ctx, axis=-1, keepdims=True)
        o = (jnp.dot(e_lat.astype(BF16), vw, preferred_element_type=F32)
             + jnp.dot(e_ctx.astype(BF16), vc, preferred_element_type=F32))
        o_ref[0, pl.ds(q0, nq), :] = (o / den).astype(BF16)
        return carry

    lax.fori_loop(0, rows_n // NA_QROWS, body, 0)


def _na_latent(p_l, p_c, rpb):
    b, l, _ = p_l.shape
    lc = p_c.shape[1]
    rows_n = l // GRID_W
    kh = NA_ROWS
    assert rpb.shape == (NA_HEADS, NA_RPB_ROWS, NA_RPB_COLS)
    assert rows_n >= NA_UNION and rows_n % NA_QROWS == 0 and (rows_n - NA_UNION) % 2 == 0

    def col(off, length):
        return pl.BlockSpec((1, length, NA_DH), lambda h, b: (b, 0, off // NA_DH + h))

    return pl.pallas_call(
        functools.partial(_na_kernel, rows_n=rows_n, kh=kh),
        grid=(NA_HEADS, b),
        in_specs=[pl.BlockSpec(memory_space=pltpu.SMEM),
                  col(OFF_NQ, l), col(OFF_NK, l), col(OFF_NV, l), col(OFF_NK, lc), col(OFF_NV, lc)],
        out_specs=pl.BlockSpec((1, l, NA_DH), lambda h, b: (b, 0, h)),
        out_shape=jax.ShapeDtypeStruct((b, l, NA_W), BF16),
        scratch_shapes=[pltpu.VMEM((NA_PAIRS, GRID_W, 2 * GRID_W), F32)],
        compiler_params=_params("parallel", "arbitrary"),
        name="na_latent",
    )(rpb.astype(F32).reshape(NA_HEADS * NA_RPB_ROWS, NA_RPB_COLS), p_l, p_l, p_l, p_c, p_c)


def _na_ctx_kernel(q_ref, k_ref, v_ref, o_ref):
    scale = NA_DH ** -0.5
    s = lax.dot_general(q_ref[0], k_ref[0], (((1,), (1,)), ((), ())), preferred_element_type=F32) * scale
    e = jnp.exp(s - jnp.max(s, axis=-1, keepdims=True))
    o = jnp.dot(e.astype(BF16), v_ref[0], preferred_element_type=F32)
    o_ref[0] = (o / jnp.sum(e, axis=-1, keepdims=True)).astype(BF16)


def _na_context(p_c):
    b, lc, _ = p_c.shape

    def col(off):
        return pl.BlockSpec((1, lc, NA_DH), lambda b, h: (b, 0, off // NA_DH + h))

    return pl.pallas_call(
        _na_ctx_kernel,
        grid=(b, NA_HEADS),
        in_specs=[col(OFF_NQ), col(OFF_NK), col(OFF_NV)],
        out_specs=pl.BlockSpec((1, lc, NA_DH), lambda b, h: (b, 0, h)),
        out_shape=jax.ShapeDtypeStruct((b, lc, NA_W), BF16),
        compiler_params=_params("parallel", "parallel"),
        name="na_context",
    )(p_c, p_c, p_c)


FFN_HALO = 16

def _ffn_kernel(x_ref, xp_ref, xn_ref, g_ref, sh_ref, sc_ref, gate_ref, upv_ref, upg_ref, wv_ref, wg_ref, bv_ref,
                bg_ref, down_ref, fg_ref, o_ref, h_ref, *, final_norm):
    i = pl.program_id(1)
    j = pl.program_id(2)
    tm = x_ref.shape[1]

    def norm_mod(x):
        ms = jnp.mean(x * x, axis=-1, keepdims=True)
        return (x * lax.rsqrt(ms + EPS) * g_ref[...]) * (1.0 + sc_ref[0]) + sh_ref[0]

    @pl.when(j == 0)
    def _():
        x = x_ref[0]
        h_ref[0:tm, :] = norm_mod(x).astype(BF16)
        hp = jnp.where(i == 0, 0.0, norm_mod(xp_ref[0]))
        hn = jnp.where(i == pl.num_programs(1) - 1, 0.0, norm_mod(xn_ref[0]))
        h_ref[tm:tm + FFN_HALO, :] = jnp.concatenate([hp, hn], axis=0).astype(BF16)
        o_ref[0] = x

    h = h_ref[...]
    row = lax.broadcasted_iota(jnp.int32, (tm, 1), 0)

    def conv3(up_ref, w_ref, b_ref):
        u = jnp.dot(h, up_ref[...], preferred_element_type=F32)
        x = u[0:tm]
        prev_row = u[tm + SUBLANES - 1:tm + SUBLANES]
        next_row = u[tm + SUBLANES:tm + SUBLANES + 1]
        x_prev = jnp.where(row == 0, prev_row, pltpu.roll(x, 1, 0))
        x_next = jnp.where(row == tm - 1, next_row, pltpu.roll(x, tm - 1, 0))
        return x_prev * w_ref[0:1, :] + x * w_ref[1:2, :] + x_next * w_ref[2:3, :] + b_ref[...]

    val = conv3(upv_ref, wv_ref, bv_ref)
    gate = conv3(upg_ref, wg_ref, bg_ref)
    act = (gate * _sigmoid(gate) * val).astype(BF16)
    o_ref[0] += gate_ref[0] * jnp.dot(act, down_ref[...], preferred_element_type=F32)

    if final_norm:
        @pl.when(j == pl.num_programs(2) - 1)
        def _():
            y = o_ref[0]
            ms = jnp.mean(y * y, axis=-1, keepdims=True)
            o_ref[0] = y * lax.rsqrt(ms + EPS) * fg_ref[...]


def _ffn(x, g, mod, layer, row, up, dw_w, dw_b, down, final_g, final_norm, name):
    b, l, d = x.shape
    f = down.shape[1]
    tm = min(l, 512)
    tf = min(f, 512)
    nf = f // tf
    rb = tm // SUBLANES

    def mspec(chunk):
        return pl.BlockSpec((None, 1, 1, d), lambda b, i, j: (layer, row(b), 0, chunk))

    def wspec(rows, off):
        return pl.BlockSpec((rows, tf), lambda b, i, j: (0, off + j))

    in_specs = [
        pl.BlockSpec((1, tm, d), lambda b, i, j: (b, i, 0)),
        pl.BlockSpec((1, SUBLANES, d), lambda b, i, j: (b, jnp.maximum(i * rb - 1, 0), 0)),
        pl.BlockSpec((1, SUBLANES, d), lambda b, i, j: (b, jnp.minimum((i + 1) * rb, l // SUBLANES - 1), 0)),
        pl.BlockSpec((1, d), lambda b, i, j: (0, 0)),
        mspec(3), mspec(4), mspec(5),
        pl.BlockSpec((None, d, tf), lambda b, i, j: (layer, 0, j)),
        pl.BlockSpec((None, d, tf), lambda b, i, j: (layer, 0, nf + j)),
        wspec(FFN_K, 0), wspec(FFN_K, nf), wspec(1, 0), wspec(1, nf),
        pl.BlockSpec((None, tf, d), lambda b, i, j: (layer, j, 0)),
        pl.BlockSpec((1, d), lambda b, i, j: (0, 0)),
    ]
    return pl.pallas_call(
        functools.partial(_ffn_kernel, final_norm=final_norm),
        grid=(b, l // tm, nf),
        in_specs=in_specs,
        out_specs=pl.BlockSpec((1, tm, d), lambda b, i, j: (b, i, 0)),
        out_shape=jax.ShapeDtypeStruct((b, l, d), F32),
        scratch_shapes=[pltpu.VMEM((tm + FFN_HALO, d), BF16)],
        compiler_params=_params("parallel", "parallel", "arbitrary"),
        name=name,
    )(x, x, x, g.reshape(1, d), mod, mod, mod, up, up, dw_w, dw_w, dw_b.reshape(1, 2 * f), dw_b.reshape(1, 2 * f),
      down, final_g.reshape(1, d))


def _rope_tables(l):
    nf = RET_DK // 4
    pos = np.arange(l)
    inv = ROPE_BASE ** (-jnp.arange(nf, dtype=F32) / nf)
    ang_r = jnp.asarray(pos // GRID_W, F32)[:, None] * inv[None, :]
    ang_c = jnp.asarray(pos % GRID_W, F32)[:, None] * inv[None, :]
    cos = jnp.concatenate([jnp.cos(ang_r), jnp.cos(ang_r), jnp.cos(ang_c), jnp.cos(ang_c)], axis=-1)
    sin = jnp.concatenate([-jnp.sin(ang_r), jnp.sin(ang_r), -jnp.sin(ang_c), jnp.sin(ang_c)], axis=-1)
    return cos, sin


def kernel(x, c, ctx, c_ctx, w_ada, b_ada, norm1_g, w_in, ret_decay, ret_gn_g, conv_dw_w, conv_dw_b, conv_ln_g,
           conv_ln_b, conv_pw, na_rpb, w_out, norm2_g, ffn_up, ffn_dw_w, ffn_dw_b, ffn_down, final_g):
    depth = w_ada.shape[0]
    bsz, l, d = x.shape

    cond = jnp.zeros((COND_ROWS, d), F32).at[:bsz].set(c).at[bsz].set(c_ctx)
    mod = _adaln(cond, w_ada, b_ada).reshape(depth, COND_ROWS, 1, 6 * d)
    lat_row = lambda b: b
    ctx_row = lambda b: bsz
    cos, sin = _rope_tables(l)
    log_gamma = jax.nn.log_sigmoid(ret_decay.astype(F32))
    w_in_b = w_in.astype(BF16)
    w_out_b = w_out.astype(BF16)
    up_b = ffn_up.astype(BF16)
    down_b = ffn_down.astype(BF16)
    tn_in = w_in.shape[-1] // 4

    h_ctx = ctx
    for layer in range(depth):
        last = layer == depth - 1
        pw_b = conv_pw[layer].astype(BF16)

        p_l = _norm_mod_matmul(x, norm1_g[layer], mod, layer, lat_row, 0, 1, w_in_b, tn_in, "in_proj")
        p_c = _norm_mod_matmul(h_ctx, norm1_g[layer], mod, layer, ctx_row, 0, 1, w_in_b, tn_in, "in_proj_ctx")

        ret_l, ret_c = _retention(p_l, p_c, log_gamma[layer], ret_gn_g[layer], cos, sin, with_ctx=not last)
        conv_l = _conv_module(p_l, conv_dw_w[layer], conv_dw_b[layer], conv_ln_g[layer], conv_ln_b[layer], pw_b)
        na_l = _na_latent(p_l, p_c, na_rpb[layer])
        x = _matmul_residual([ret_l, conv_l, na_l], w_out_b, x, mod, layer, lat_row, 2, 1024, "out_proj")

        x = _ffn(x, norm2_g[layer], mod, layer, lat_row, up_b, ffn_dw_w[layer], ffn_dw_b[layer], down_b, final_g,
                 last, "ffn")

        if not last:
            conv_c = _conv_module(p_c, conv_dw_w[layer], conv_dw_b[layer], conv_ln_g[layer], conv_ln_b[layer], pw_b)
            na_c = _na_context(p_c)
            h_ctx = _matmul_residual([ret_c, conv_c, na_c], w_out_b, h_ctx, mod, layer, ctx_row, 2, 1024,
                                     "out_proj_ctx")
            h_ctx = _ffn(h_ctx, norm2_g[layer], mod, layer, ctx_row, up_b, ffn_dw_w[layer], ffn_dw_b[layer], down_b,
                         final_g, False, "ffn_ctx")

    return x
```

```python
import functools

import numpy as np
import jax
import jax.numpy as jnp
from jax import lax
from jax.experimental import pallas as pl
from jax.experimental.pallas import tpu as pltpu

F32 = jnp.float32
BF16 = jnp.bfloat16

GRID_W = 64
RET_HEADS = 4
RET_DK = 128
RET_DV = 256
RET_CHUNK = 128
RET_UNROLL = 8
CONV_W = 512
CONV_K = 31
NA_HEADS = 4
NA_DH = 128
NA_ROWS = 8
NA_COLS = 16
FFN_K = 3
ROPE_BASE = 10000.0
EPS = 1e-6

RET_QK_W = RET_HEADS * RET_DK
RET_W = RET_HEADS * RET_DV
NA_W = NA_HEADS * NA_DH
OFF_RQ = 0
OFF_RK = OFF_RQ + RET_QK_W
OFF_RV = OFF_RK + RET_QK_W
OFF_RG = OFF_RV + RET_W
OFF_CA = OFF_RG + RET_W
OFF_CB = OFF_CA + CONV_W
OFF_NQ = OFF_CB + CONV_W
OFF_NK = OFF_NQ + NA_W
OFF_NV = OFF_NK + NA_W

V7X_VMEM_BYTES = 64 * 1024 * 1024
VMEM_LIMIT = V7X_VMEM_BYTES - 8 * 1024 * 1024
SUBLANES = 8
LANES = 128
COND_ROWS = 8
MASK_VALUE = -1e30
CONV_PAD = 16
CONV_TILE = 64


def _sigmoid(x):
    return 1.0 / (1.0 + jnp.exp(-x))


def _params(*sem):
    return pltpu.CompilerParams(dimension_semantics=sem, vmem_limit_bytes=VMEM_LIMIT)


def _adaln_kernel(c_ref, w_ref, b_ref, o_ref):
    c = c_ref[...]
    s = (c * _sigmoid(c)).astype(BF16)
    o_ref[0] = jnp.dot(s, w_ref[0].astype(BF16), preferred_element_type=F32) + b_ref[0]


def _adaln(cond, w_ada, b_ada):
    depth, d, n = w_ada.shape
    tn = 1024
    return pl.pallas_call(
        _adaln_kernel,
        grid=(depth, n // tn),
        in_specs=[
            pl.BlockSpec((COND_ROWS, d), lambda l, j: (0, 0)),
            pl.BlockSpec((1, d, tn), lambda l, j: (l, 0, j)),
            pl.BlockSpec((1, 1, tn), lambda l, j: (l, 0, j)),
        ],
        out_specs=pl.BlockSpec((1, COND_ROWS, tn), lambda l, j: (l, 0, j)),
        out_shape=jax.ShapeDtypeStruct((depth, COND_ROWS, n), F32),
        compiler_params=_params("parallel", "parallel"),
        name="adaln",
    )(cond, w_ada, b_ada.reshape(depth, 1, n))


def _mod_spec(layer, d, chunk, row):
    return pl.BlockSpec((None, 1, 1, d), lambda b, i, j: (layer, row(b), 0, chunk))


def _norm_mod_matmul_kernel(x_ref, g_ref, sh_ref, sc_ref, w_ref, o_ref, h_ref):
    @pl.when(pl.program_id(2) == 0)
    def _():
        x = x_ref[0]
        ms = jnp.mean(x * x, axis=-1, keepdims=True)
        y = x * lax.rsqrt(ms + EPS) * g_ref[...]
        h_ref[...] = (y * (1.0 + sc_ref[0]) + sh_ref[0]).astype(BF16)

    o_ref[0] = jnp.dot(h_ref[...], w_ref[...], preferred_element_type=F32).astype(o_ref.dtype)


def _norm_mod_matmul(x, g, mod, layer, row, shift_chunk, scale_chunk, w, tn, name):
    b, l, d = x.shape
    n = w.shape[-1]
    tm = min(l, 1024)
    return pl.pallas_call(
        _norm_mod_matmul_kernel,
        grid=(b, l // tm, n // tn),
        in_specs=[
            pl.BlockSpec((1, tm, d), lambda b, i, j: (b, i, 0)),
            pl.BlockSpec((1, d), lambda b, i, j: (0, 0)),
            _mod_spec(layer, d, shift_chunk, row),
            _mod_spec(layer, d, scale_chunk, row),
            pl.BlockSpec((None, d, tn), lambda b, i, j: (layer, 0, j)),
        ],
        out_specs=pl.BlockSpec((1, tm, tn), lambda b, i, j: (b, i, j)),
        out_shape=jax.ShapeDtypeStruct((b, l, n), BF16),
        scratch_shapes=[pltpu.VMEM((tm, d), BF16)],
        compiler_params=_params("parallel", "parallel", "arbitrary"),
        name=name,
    )(x, g.reshape(1, d), mod, mod, w)


def _matmul_residual_kernel(*refs, n_a):
    a_refs = refs[:n_a]
    w_ref, res_ref, gate_ref, o_ref = refs[n_a:]
    acc = None
    off = 0
    for a_ref in a_refs:
        k = a_ref.shape[-1]
        part = jnp.dot(a_ref[0], w_ref[off:off + k, :], preferred_element_type=F32)
        acc = part if acc is None else acc + part
        off += k
    o_ref[0] = res_ref[0] + gate_ref[0] * acc


def _matmul_residual(a_list, w, res, mod, layer, row, gate_chunk, tn, name):
    b, l, n = res.shape
    k_total = w.shape[1]
    tm = min(l, 1024)
    tn = min(tn, n)
    in_specs = [pl.BlockSpec((1, tm, a.shape[-1]), lambda b, i, j: (b, i, 0)) for a in a_list]
    in_specs += [
        pl.BlockSpec((None, k_total, tn), lambda b, i, j: (layer, 0, j)),
        pl.BlockSpec((1, tm, tn), lambda b, i, j: (b, i, j)),
        pl.BlockSpec((None, 1, 1, tn), lambda b, i, j: (layer, row(b), 0, gate_chunk * (n // tn) + j)),
    ]
    return pl.pallas_call(
        functools.partial(_matmul_residual_kernel, n_a=len(a_list)),
        grid=(b, l // tm, n // tn),
        in_specs=in_specs,
        out_specs=pl.BlockSpec((1, tm, tn), lambda b, i, j: (b, i, j)),
        out_shape=jax.ShapeDtypeStruct((b, l, n), F32),
        compiler_params=_params("parallel", "parallel", "arbitrary"),
        name=name,
    )(*a_list, w, res, mod)


def _rope(t, cos, sin):
    lane = lax.broadcasted_iota(jnp.int32, t.shape, 1)
    partner = jnp.where((lane & 32) == 0, pltpu.roll(t, LANES - 32, 1), pltpu.roll(t, 32, 1))
    return t * cos + partner * sin


def _retention_kernel(lg_ref, ql_ref, kl_ref, vl_ref, gl_ref, qc_ref, kc_ref, vc_ref, gc_ref, cos_ref, sin_ref,
                      gn_ref, *rest, with_ctx, n_lat, n_ctx):
    if with_ctx:
        ol_ref, oc_ref, acc_ref, qr_ref, kr_ref, accc_ref = rest
    else:
        ol_ref, acc_ref, qr_ref, kr_ref = rest
    c = RET_CHUNK
    scale = RET_DK ** -0.5
    h = pl.program_id(1)
    lgf = lg_ref[0, h]
    lgb = lg_ref[1, h]
    ii = lax.broadcasted_iota(jnp.int32, (c, c), 0).astype(F32)
    jj = lax.broadcasted_iota(jnp.int32, (c, c), 1).astype(F32)
    diff = ii - jj
    decay = (jnp.where(diff >= 0, jnp.exp(jnp.maximum(diff, 0.0) * lgf), 0.0)
             + jnp.where(diff <= 0, jnp.exp(jnp.maximum(-diff, 0.0) * lgb), 0.0))
    ic = lax.broadcasted_iota(jnp.int32, (c, 1), 0).astype(F32)
    xi_f = jnp.exp((ic + 1.0) * lgf)
    zeta_f = jnp.exp((c - 1.0 - ic) * lgf)
    xi_b = jnp.exp((c - ic) * lgb)
    zeta_b = jnp.exp(ic * lgb)
    g_f = jnp.exp(jnp.full((1, RET_DV), float(c), F32) * lgf)
    g_b = jnp.exp(jnp.full((1, RET_DV), float(c), F32) * lgb)
    gn = gn_ref[...]

    def intra_and_inter(qb, kb, v, xi, st):
        s = lax.dot_general(qb, kb, (((1,), (1,)), ((), ())), preferred_element_type=F32)
        p = (s * decay).astype(BF16)
        inter = jnp.dot(qb, st.astype(BF16), preferred_element_type=F32) * xi
        return jnp.dot(p, v, preferred_element_type=F32) + inter

    def next_state(st, k_f32, v, zeta, g):
        kz = (k_f32 * zeta).T.astype(BF16)
        return st * g + jnp.dot(kz, v, preferred_element_type=F32)

    def finish(total, gate):
        mu = jnp.mean(total, axis=-1, keepdims=True)
        d = total - mu
        var = jnp.mean(d * d, axis=-1, keepdims=True)
        y = d * lax.rsqrt(var + EPS) * gn
        return (y * (gate * _sigmoid(gate))).astype(BF16)

    st = jnp.zeros((RET_DK, RET_DV), F32)
    for j in range(n_ctx):
        rows = slice(j * c, (j + 1) * c)
        q = qc_ref[0, rows, :].astype(F32) * scale
        k = kc_ref[0, rows, :].astype(F32)
        v = vc_ref[0, rows, :]
        if with_ctx:
            accc_ref[rows, :] = intra_and_inter(q.astype(BF16), k.astype(BF16), v, xi_f, st)
        st = next_state(st, k, v, zeta_f, g_f)

    nt = (((1,), (1,)), ((), ()))
    group = RET_UNROLL
    assert n_lat % group == 0

    def fwd_body(jg, st):
        rows, qb, kb, v, kzv = [], [], [], [], []
        for u in range(group):
            r = pl.ds(pl.multiple_of((jg * group + u) * c, c), c)
            cos = cos_ref[r, :]
            sin = sin_ref[r, :]
            q = _rope(ql_ref[0, r, :].astype(F32), cos, sin) * scale
            k = _rope(kl_ref[0, r, :].astype(F32), cos, sin)
            rows.append(r)
            qb.append(q.astype(BF16))
            kb.append(k.astype(BF16))
            v.append(vl_ref[0, r, :])
            qr_ref[r, :] = qb[u]
            kr_ref[r, :] = kb[u]
            kzv.append(jnp.dot((k * zeta_f).T.astype(BF16), v[u], preferred_element_type=F32))
        s = [lax.dot_general(qb[u], kb[u], nt, preferred_element_type=F32) for u in range(group)]
        states = []
        for u in range(group):
            states.append(st.astype(BF16))
            st = st * g_f + kzv[u]
        p = [(s[u] * decay).astype(BF16) for u in range(group)]
        for u in range(group):
            inter = jnp.dot(qb[u], states[u], preferred_element_type=F32) * xi_f
            acc_ref[rows[u], :] = jnp.dot(p[u], v[u], preferred_element_type=F32) + inter
        return st

    lax.fori_loop(0, n_lat // group, fwd_body, st)

    st = jnp.zeros((RET_DK, RET_DV), F32)
    for j in reversed(range(n_ctx)):
        rows = slice(j * c, (j + 1) * c)
        k = kc_ref[0, rows, :].astype(F32)
        v = vc_ref[0, rows, :]
        if with_ctx:
            qb = (qc_ref[0, rows, :].astype(F32) * scale).astype(BF16)
            inter = jnp.dot(qb, st.astype(BF16), preferred_element_type=F32) * xi_b
            oc_ref[0, rows, :] = finish(accc_ref[rows, :] + inter, gc_ref[0, rows, :].astype(F32))
        st = next_state(st, k, v, zeta_b, g_b)

    def bwd_body(jg, st):
        rows, kzv = [], []
        for u in range(group):
            r = pl.ds(pl.multiple_of((n_lat - 1 - (jg * group + u)) * c, c), c)
            rows.append(r)
            kz = (kr_ref[r, :].astype(F32) * zeta_b).T.astype(BF16)
            kzv.append(jnp.dot(kz, vl_ref[0, r, :], preferred_element_type=F32))
        states = []
        for u in range(group):
            states.append(st.astype(BF16))
            st = st * g_b + kzv[u]
        inter = [jnp.dot(qr_ref[rows[u], :], states[u], preferred_element_type=F32) * xi_b for u in range(group)]
        for u in range(group):
            ol_ref[0, rows[u], :] = finish(acc_ref[rows[u], :] + inter[u], gl_ref[0, rows[u], :].astype(F32))
        return st

    lax.fori_loop(0, n_lat // group, bwd_body, st)


def _retention(p_l, p_c, log_gamma, gn_g, cos, sin, with_ctx):
    b, l, _ = p_l.shape
    lc = p_c.shape[1]
    c = RET_CHUNK

    def col(width, off, length):
        return pl.BlockSpec((1, length, width), lambda b, h: (b, 0, off // width + h))

    in_specs = [
        pl.BlockSpec(memory_space=pltpu.SMEM),
        col(RET_DK, OFF_RQ, l), col(RET_DK, OFF_RK, l), col(RET_DV, OFF_RV, l), col(RET_DV, OFF_RG, l),
        col(RET_DK, OFF_RQ, lc), col(RET_DK, OFF_RK, lc), col(RET_DV, OFF_RV, lc), col(RET_DV, OFF_RG, lc),
        pl.BlockSpec((l, RET_DK), lambda b, h: (0, 0)),
        pl.BlockSpec((l, RET_DK), lambda b, h: (0, 0)),
        pl.BlockSpec((1, RET_DV), lambda b, h: (0, h)),
    ]
    out_specs = [pl.BlockSpec((1, l, RET_DV), lambda b, h: (b, 0, h))]
    out_shape = [jax.ShapeDtypeStruct((b, l, RET_W), BF16)]
    scratch = [pltpu.VMEM((l, RET_DV), F32), pltpu.VMEM((l, RET_DK), BF16), pltpu.VMEM((l, RET_DK), BF16)]
    if with_ctx:
        out_specs.append(pl.BlockSpec((1, lc, RET_DV), lambda b, h: (b, 0, h)))
        out_shape.append(jax.ShapeDtypeStruct((b, lc, RET_W), BF16))
        scratch.append(pltpu.VMEM((lc, RET_DV), F32))
    outs = pl.pallas_call(
        functools.partial(_retention_kernel, with_ctx=with_ctx, n_lat=l // c, n_ctx=lc // c),
        grid=(b, RET_HEADS),
        in_specs=in_specs,
        out_specs=out_specs,
        out_shape=out_shape,
        scratch_shapes=scratch,
        compiler_params=_params("parallel", "parallel"),
        name="retention_ctx" if with_ctx else "retention",
    )(log_gamma, p_l, p_l, p_l, p_l, p_c, p_c, p_c, p_c, cos, sin, gn_g.reshape(1, RET_W))
    return (outs[0], outs[1]) if with_ctx else (outs[0], None)


def _conv_kernel(a_ref, b_ref, dww_ref, dwb_ref, lng_ref, lnb_ref, pw_ref, o_ref, upad_ref, act_ref, *, l):
    pad = CONV_PAD
    upad_ref[0:pad, :] = jnp.zeros((pad, CONV_W), F32)
    upad_ref[pad + l:pad + l + pad, :] = jnp.zeros((pad, CONV_W), F32)
    tg = min(l, 256)

    def glu_body(t, carry):
        r0 = pl.multiple_of(t * tg, tg)
        a = a_ref[0, pl.ds(r0, tg), :].astype(F32)
        b = b_ref[0, pl.ds(r0, tg), :].astype(F32)
        upad_ref[pl.ds(pad + r0, tg), :] = a * _sigmoid(b)
        return carry

    lax.fori_loop(0, l // tg, glu_body, 0)

    tile = CONV_TILE
    halo = 2 * pad
    lng = lng_ref[...]
    lnb = lnb_ref[...]

    def conv_body(t, carry):
        r0 = pl.multiple_of(t * tile, tile)
        ys = []
        for cb in range(CONV_W // LANES):
            cols = slice(cb * LANES, (cb + 1) * LANES)
            win = upad_ref[pl.ds(r0, tile + halo), cols]
            acc = jnp.zeros((tile, LANES), F32) + dwb_ref[:, cols]
            for r in range(SUBLANES):
                rolled = win if r == 0 else pltpu.roll(win, tile + halo - r, 0)
                for a in range(halo // SUBLANES):
                    k = SUBLANES * a + r - 1
                    if 0 <= k < CONV_K:
                        acc = acc + rolled[SUBLANES * a:SUBLANES * a + tile] * dww_ref[k:k + 1, cols]
            ys.append(acc)
        y = jnp.concatenate(ys, axis=-1)
        mu = jnp.mean(y, axis=-1, keepdims=True)
        d = y - mu
        var = jnp.mean(d * d, axis=-1, keepdims=True)
        u = d * lax.rsqrt(var + EPS) * lng + lnb
        act_ref[pl.ds(r0, tile), :] = (u * _sigmoid(u)).astype(BF16)
        return carry

    lax.fori_loop(0, l // tile, conv_body, 0)

    tmm = min(l, 512)

    def mm_body(t, carry):
        r0 = pl.multiple_of(t * tmm, tmm)
        o_ref[0, pl.ds(r0, tmm), :] = jnp.dot(act_ref[pl.ds(r0, tmm), :], pw_ref[...],
                                              preferred_element_type=F32).astype(BF16)
        return carry

    lax.fori_loop(0, l // tmm, mm_body, 0)


def _conv_module(p, dw_w, dw_b, ln_g, ln_b, pw):
    b, l, _ = p.shape
    vec = pl.BlockSpec((1, CONV_W), lambda b: (0, 0))
    return pl.pallas_call(
        functools.partial(_conv_kernel, l=l),
        grid=(b,),
        in_specs=[
            pl.BlockSpec((1, l, CONV_W), lambda b: (b, 0, OFF_CA // CONV_W)),
            pl.BlockSpec((1, l, CONV_W), lambda b: (b, 0, OFF_CB // CONV_W)),
            pl.BlockSpec((CONV_K, CONV_W), lambda b: (0, 0)),
            vec, vec, vec,
            pl.BlockSpec((CONV_W, CONV_W), lambda b: (0, 0)),
        ],
        out_specs=pl.BlockSpec((1, l, CONV_W), lambda b: (b, 0, 0)),
        out_shape=jax.ShapeDtypeStruct((b, l, CONV_W), BF16),
        scratch_shapes=[pltpu.VMEM((l + 2 * CONV_PAD, CONV_W), F32),
                        pltpu.VMEM((l, CONV_W), BF16)],
        compiler_params=_params("parallel"),
        name="conv_module",
    )(p, p, dw_w, dw_b.reshape(1, CONV_W), ln_g.reshape(1, CONV_W), ln_b.reshape(1, CONV_W), pw)


NA_RPB_ROWS = 2 * NA_ROWS - 1
NA_RPB_COLS = 2 * NA_COLS - 1
NA_PAIRS = NA_RPB_ROWS - 1
NA_QROWS = 4
NA_UNION = NA_ROWS + NA_QROWS
NA_GROUP = 2


def _na_kernel(rpb_ref, q_ref, k_ref, v_ref, kc_ref, vc_ref, o_ref, pairs_ref, *, rows_n, kh):
    scale = NA_DH ** -0.5
    h = pl.program_id(0)
    nt = (((1,), (1,)), ((), ()))

    @pl.when(pl.program_id(1) == 0)
    def _build_bias():
        lane = lax.broadcasted_iota(jnp.int32, (GRID_W, 2 * GRID_W), 1)
        qcol = lax.broadcasted_iota(jnp.int32, (GRID_W, 2 * GRID_W), 0)
        w = lane & (GRID_W - 1)
        col_off = jnp.clip(w - qcol + (NA_COLS - 1), 0, NA_RPB_COLS - 1)
        col_start = jnp.clip(qcol - NA_COLS // 2, 0, GRID_W - NA_COLS)
        col_in = (w >= col_start) & (w < col_start + NA_COLS)
        left = lane < GRID_W

        def pair_body(k, carry):
            row = h * NA_RPB_ROWS + k
            acc = jnp.zeros((GRID_W, 2 * GRID_W), F32)
            for d in range(NA_RPB_COLS):
                val = jnp.where(left, rpb_ref[row, d], rpb_ref[row + 1, d])
                acc = jnp.where(col_off == d, val, acc)
            pairs_ref[k] = jnp.where(col_in, acc, MASK_VALUE)
            return carry

        lax.fori_loop(0, NA_PAIRS, pair_body, 0)

    kc = kc_ref[0]
    vc = vc_ref[0]

    lane = lax.broadcasted_iota(jnp.int32, (GRID_W, 2 * GRID_W), 1)
    left = lane < GRID_W
    nq = NA_QROWS * GRID_W
    nk = NA_UNION * GRID_W

    def scores(g):
        r0 = g * NA_QROWS
        u0 = jnp.clip(r0 - kh // 2, 0, rows_n - NA_UNION)
        q0 = pl.multiple_of(r0 * GRID_W, nq)
        k0 = pl.multiple_of(u0 * GRID_W, GRID_W)
        q = q_ref[0, pl.ds(q0, nq), :]
        kw = k_ref[0, pl.ds(k0, nk), :]
        vw = v_ref[0, pl.ds(k0, nk), :]
        bias_rows = []
        for rq in range(NA_QROWS):
            r = r0 + rq
            s0 = jnp.clip(r - kh // 2, 0, rows_n - kh)
            tiles = []
            for m in range(NA_UNION // 2):
                kr = u0 + 2 * m
                ok_l = (kr >= s0) & (kr < s0 + kh)
                ok_r = (kr + 1 >= s0) & (kr + 1 < s0 + kh)
                ok = jnp.where(left, ok_l.astype(jnp.int32), ok_r.astype(jnp.int32)) != 0
                pair = pairs_ref[jnp.clip(kr - r + (NA_ROWS - 1), 0, NA_PAIRS - 1)]
                tiles.append(jnp.where(ok, pair, MASK_VALUE))
            bias_rows.append(jnp.concatenate(tiles, axis=-1))
        bias = jnp.concatenate(bias_rows, axis=0)
        s_lat = lax.dot_general(q, kw, nt, preferred_element_type=F32) * scale + bias
        s_ctx = lax.dot_general(q, kc, nt, preferred_element_type=F32) * scale
        return s_lat, s_ctx, vw, q0

    def weights(s_lat, s_ctx):
        m = jnp.maximum(jnp.max(s_lat, axis=-1, keepdims=True), jnp.max(s_ctx, axis=-1, keepdims=True))
        e_lat = jnp.exp(s_lat - m)
        e_ctx = jnp.exp(s_ctx - m)
        den = jnp.sum(e_lat, axis=-1, keepdims=True) + jnp.sum(e_ctx, axis=-1, keepdims=True)
        return e_lat.astype(BF16), e_ctx.astype(BF16), den

    def body(t, carry):
        sc = [scores(t * NA_GROUP + u) for u in range(NA_GROUP)]
        ws = [weights(s[0], s[1]) for s in sc]
        for (s_lat, s_ctx, vw, q0), (e_lat, e_ctx, den) in zip(sc, ws):
            o = jnp.dot(e_lat, vw, preferred_element_type=F32) + jnp.dot(e_ctx, vc, preferred_element_type=F32)
            o_ref[0, pl.ds(q0, nq), :] = (o / den).astype(BF16)
        return carry

    lax.fori_loop(0, rows_n // (NA_QROWS * NA_GROUP), body, 0)


def _na_latent(p_l, p_c, rpb):
    b, l, _ = p_l.shape
    lc = p_c.shape[1]
    rows_n = l // GRID_W
    kh = NA_ROWS
    assert rpb.shape == (NA_HEADS, NA_RPB_ROWS, NA_RPB_COLS)
    assert rows_n >= NA_UNION and rows_n % (NA_QROWS * NA_GROUP) == 0 and (rows_n - NA_UNION) % 2 == 0

    def col(off, length):
        return pl.BlockSpec((1, length, NA_DH), lambda h, b: (b, 0, off // NA_DH + h))

    return pl.pallas_call(
        functools.partial(_na_kernel, rows_n=rows_n, kh=kh),
        grid=(NA_HEADS, b),
        in_specs=[pl.BlockSpec(memory_space=pltpu.SMEM),
                  col(OFF_NQ, l), col(OFF_NK, l), col(OFF_NV, l), col(OFF_NK, lc), col(OFF_NV, lc)],
        out_specs=pl.BlockSpec((1, l, NA_DH), lambda h, b: (b, 0, h)),
        out_shape=jax.ShapeDtypeStruct((b, l, NA_W), BF16),
        scratch_shapes=[pltpu.VMEM((NA_PAIRS, GRID_W, 2 * GRID_W), F32)],
        compiler_params=_params("parallel", "arbitrary"),
        name="na_latent",
    )(rpb.astype(F32).reshape(NA_HEADS * NA_RPB_ROWS, NA_RPB_COLS), p_l, p_l, p_l, p_c, p_c)


def _na_ctx_kernel(q_ref, k_ref, v_ref, o_ref):
    scale = NA_DH ** -0.5
    s = lax.dot_general(q_ref[0], k_ref[0], (((1,), (1,)), ((), ())), preferred_element_type=F32) * scale
    e = jnp.exp(s - jnp.max(s, axis=-1, keepdims=True))
    o = jnp.dot(e.astype(BF16), v_ref[0], preferred_element_type=F32)
    o_ref[0] = (o / jnp.sum(e, axis=-1, keepdims=True)).astype(BF16)


def _na_context(p_c):
    b, lc, _ = p_c.shape

    def col(off):
        return pl.BlockSpec((1, lc, NA_DH), lambda b, h: (b, 0, off // NA_DH + h))

    return pl.pallas_call(
        _na_ctx_kernel,
        grid=(b, NA_HEADS),
        in_specs=[col(OFF_NQ), col(OFF_NK), col(OFF_NV)],
        out_specs=pl.BlockSpec((1, lc, NA_DH), lambda b, h: (b, 0, h)),
        out_shape=jax.ShapeDtypeStruct((b, lc, NA_W), BF16),
        compiler_params=_params("parallel", "parallel"),
        name="na_context",
    )(p_c, p_c, p_c)


FFN_HALO = 16
FFN_SLICE = 256
def _ffn_kernel(x_ref, xp_ref, xn_ref, g_ref, sh_ref, sc_ref, gate_ref, upv_ref, upg_ref, wv_ref, wg_ref, bv_ref,
                bg_ref, down_ref, fg_ref, o_ref, h_ref, *, final_norm):
    i = pl.program_id(1)
    j = pl.program_id(2)
    tm = x_ref.shape[1]

    def norm_mod(x):
        ms = jnp.mean(x * x, axis=-1, keepdims=True)
        return (x * lax.rsqrt(ms + EPS) * g_ref[...]) * (1.0 + sc_ref[0]) + sh_ref[0]

    @pl.when(j == 0)
    def _():
        h_ref[0:tm, :] = norm_mod(x_ref[0]).astype(BF16)
        hp = jnp.where(i == 0, 0.0, norm_mod(xp_ref[0]))
        hn = jnp.where(i == pl.num_programs(1) - 1, 0.0, norm_mod(xn_ref[0]))
        h_ref[tm:tm + FFN_HALO, :] = jnp.concatenate([hp, hn], axis=0).astype(BF16)
        o_ref[0] = jnp.zeros(o_ref.shape[1:], F32)

    h = h_ref[...]
    edge = lax.broadcasted_iota(jnp.int32, (SUBLANES, 1), 0)

    def conv3(u, w_ref, b_ref, c):
        x = u[0:tm]
        prev_row = u[tm + SUBLANES - 1:tm + SUBLANES]
        next_row = u[tm + SUBLANES:tm + SUBLANES + 1]
        x_prev = pltpu.roll(x, 1, 0)
        x_next = pltpu.roll(x, tm - 1, 0)
        x_prev = jnp.concatenate([jnp.where(edge == 0, prev_row, x_prev[0:SUBLANES]), x_prev[SUBLANES:]], axis=0)
        x_next = jnp.concatenate([x_next[:tm - SUBLANES],
                                  jnp.where(edge == SUBLANES - 1, next_row, x_next[tm - SUBLANES:])], axis=0)
        return x_prev * w_ref[0:1, c] + x * w_ref[1:2, c] + x_next * w_ref[2:3, c] + b_ref[:, c]

    tf = upv_ref.shape[1]
    width = min(tf, FFN_SLICE)
    slices = [slice(s * width, (s + 1) * width) for s in range(tf // width)]
    us = []
    for c in slices:
        us.append((jnp.dot(h, upv_ref[:, c], preferred_element_type=F32),
                   jnp.dot(h, upg_ref[:, c], preferred_element_type=F32)))
    for (u_val, u_gate), c in zip(us, slices):
        val = conv3(u_val, wv_ref, bv_ref, c)
        gate = conv3(u_gate, wg_ref, bg_ref, c)
        act = (gate * _sigmoid(gate) * val).astype(BF16)
        o_ref[0] += jnp.dot(act, down_ref[c, :], preferred_element_type=F32)

    @pl.when(j == pl.num_programs(2) - 1)
    def _():
        y = x_ref[0] + gate_ref[0] * o_ref[0]
        if final_norm:
            ms = jnp.mean(y * y, axis=-1, keepdims=True)
            y = y * lax.rsqrt(ms + EPS) * fg_ref[...]
        o_ref[0] = y


def _ffn(x, g, mod, layer, row, up, dw_w, dw_b, down, final_g, final_norm, name):
    b, l, d = x.shape
    f = down.shape[1]
    tm = min(l, 512)
    tf = min(f, 512)
    nf = f // tf
    rb = tm // SUBLANES

    def mspec(chunk):
        return pl.BlockSpec((None, 1, 1, d), lambda b, i, j: (layer, row(b), 0, chunk))

    def wspec(rows, off):
        return pl.BlockSpec((rows, tf), lambda b, i, j: (0, off + j))

    in_specs = [
        pl.BlockSpec((1, tm, d), lambda b, i, j: (b, i, 0)),
        pl.BlockSpec((1, SUBLANES, d), lambda b, i, j: (b, jnp.maximum(i * rb - 1, 0), 0)),
        pl.BlockSpec((1, SUBLANES, d), lambda b, i, j: (b, jnp.minimum((i + 1) * rb, l // SUBLANES - 1), 0)),
        pl.BlockSpec((1, d), lambda b, i, j: (0, 0)),
        mspec(3), mspec(4), mspec(5),
        pl.BlockSpec((None, d, tf), lambda b, i, j: (layer, 0, j)),
        pl.BlockSpec((None, d, tf), lambda b, i, j: (layer, 0, nf + j)),
        wspec(FFN_K, 0), wspec(FFN_K, nf), wspec(1, 0), wspec(1, nf),
        pl.BlockSpec((None, tf, d), lambda b, i, j: (layer, j, 0)),
        pl.BlockSpec((1, d), lambda b, i, j: (0, 0)),
    ]
    return pl.pallas_call(
        functools.partial(_ffn_kernel, final_norm=final_norm),
        grid=(b, l // tm, nf),
        in_specs=in_specs,
        out_specs=pl.BlockSpec((1, tm, d), lambda b, i, j: (b, i, 0)),
        out_shape=jax.ShapeDtypeStruct((b, l, d), F32),
        scratch_shapes=[pltpu.VMEM((tm + FFN_HALO, d), BF16)],
        compiler_params=_params("parallel", "parallel", "arbitrary"),
        name=name,
    )(x, x, x, g.reshape(1, d), mod, mod, mod, up, up, dw_w, dw_w, dw_b.reshape(1, 2 * f), dw_b.reshape(1, 2 * f),
      down, final_g.reshape(1, d))


def _rope_tables(l):
    nf = RET_DK // 4
    pos = np.arange(l)
    inv = ROPE_BASE ** (-jnp.arange(nf, dtype=F32) / nf)
    ang_r = jnp.asarray(pos // GRID_W, F32)[:, None] * inv[None, :]
    ang_c = jnp.asarray(pos % GRID_W, F32)[:, None] * inv[None, :]
    cos = jnp.concatenate([jnp.cos(ang_r), jnp.cos(ang_r), jnp.cos(ang_c), jnp.cos(ang_c)], axis=-1)
    sin = jnp.concatenate([-jnp.sin(ang_r), jnp.sin(ang_r), -jnp.sin(ang_c), jnp.sin(ang_c)], axis=-1)
    return cos, sin


def kernel(x, c, ctx, c_ctx, w_ada, b_ada, norm1_g, w_in, ret_decay, ret_gn_g, conv_dw_w, conv_dw_b, conv_ln_g,
           conv_ln_b, conv_pw, na_rpb, w_out, norm2_g, ffn_up, ffn_dw_w, ffn_dw_b, ffn_down, final_g):
    depth = w_ada.shape[0]
    bsz, l, d = x.shape

    cond = jnp.zeros((COND_ROWS, d), F32).at[:bsz].set(c).at[bsz].set(c_ctx)
    mod = _adaln(cond, w_ada, b_ada).reshape(depth, COND_ROWS, 1, 6 * d)
    lat_row = lambda b: b
    ctx_row = lambda b: bsz
    cos, sin = _rope_tables(l)
    log_gamma = jax.nn.log_sigmoid(ret_decay.astype(F32))
    w_in_b = w_in.astype(BF16)
    w_out_b = w_out.astype(BF16)
    up_b = ffn_up.astype(BF16)
    down_b = ffn_down.astype(BF16)
    tn_in = w_in.shape[-1] // 4

    h_ctx = ctx
    for layer in range(depth):
        last = layer == depth - 1
        pw_b = conv_pw[layer].astype(BF16)

        p_l = _norm_mod_matmul(x, norm1_g[layer], mod, layer, lat_row, 0, 1, w_in_b, tn_in, "in_proj")
        p_c = _norm_mod_matmul(h_ctx, norm1_g[layer], mod, layer, ctx_row, 0, 1, w_in_b, tn_in, "in_proj_ctx")

        ret_l, ret_c = _retention(p_l, p_c, log_gamma[layer], ret_gn_g[layer], cos, sin, with_ctx=not last)
        conv_l = _conv_module(p_l, conv_dw_w[layer], conv_dw_b[layer], conv_ln_g[layer], conv_ln_b[layer], pw_b)
        na_l = _na_latent(p_l, p_c, na_rpb[layer])
        x = _matmul_residual([ret_l, conv_l, na_l], w_out_b, x, mod, layer, lat_row, 2, 1024, "out_proj")

        x = _ffn(x, norm2_g[layer], mod, layer, lat_row, up_b, ffn_dw_w[layer], ffn_dw_b[layer], down_b, final_g,
                 last, "ffn")

        if not last:
            conv_c = _conv_module(p_c, conv_dw_w[layer], conv_dw_b[layer], conv_ln_g[layer], conv_ln_b[layer], pw_b)
            na_c = _na_context(p_c)
            h_ctx = _matmul_residual([ret_c, conv_c, na_c], w_out_b, h_ctx, mod, layer, ctx_row, 2, 1024,
                                     "out_proj_ctx")
            h_ctx = _ffn(h_ctx, norm2_g[layer], mod, layer, ctx_row, up_b, ffn_dw_w[layer], ffn_dw_b[layer], down_b,
                         final_g, False, "ffn_ctx")

    return x
```

```python
import functools

import numpy as np
import jax
import jax.numpy as jnp
from jax import lax
from jax.experimental import pallas as pl
from jax.experimental.pallas import tpu as pltpu

F32 = jnp.float32
BF16 = jnp.bfloat16

GRID_W = 64
RET_HEADS = 4
RET_DK = 128
RET_DV = 256
RET_CHUNK = 128
RET_UNROLL = 8
CONV_W = 512
CONV_K = 31
NA_HEADS = 4
NA_DH = 128
NA_ROWS = 8
NA_COLS = 16
FFN_K = 3
ROPE_BASE = 10000.0
EPS = 1e-6

RET_QK_W = RET_HEADS * RET_DK
RET_W = RET_HEADS * RET_DV
NA_W = NA_HEADS * NA_DH
OFF_RQ = 0
OFF_RK = OFF_RQ + RET_QK_W
OFF_RV = OFF_RK + RET_QK_W
OFF_RG = OFF_RV + RET_W
OFF_CA = OFF_RG + RET_W
OFF_CB = OFF_CA + CONV_W
OFF_NQ = OFF_CB + CONV_W
OFF_NK = OFF_NQ + NA_W
OFF_NV = OFF_NK + NA_W

V7X_VMEM_BYTES = 64 * 1024 * 1024
VMEM_LIMIT = V7X_VMEM_BYTES - 8 * 1024 * 1024
SUBLANES = 8
LANES = 128
COND_ROWS = 8
MASK_VALUE = -1e30
CONV_PAD = 16
CONV_TILE = 64


def _sigmoid(x):
    return 1.0 / (1.0 + jnp.exp(-x))


def _params(*sem):
    return pltpu.CompilerParams(dimension_semantics=sem, vmem_limit_bytes=VMEM_LIMIT)


def _adaln_kernel(c_ref, w_ref, b_ref, o_ref):
    c = c_ref[...]
    s = (c * _sigmoid(c)).astype(BF16)
    o_ref[0] = jnp.dot(s, w_ref[0].astype(BF16), preferred_element_type=F32) + b_ref[0]


def _adaln(cond, w_ada, b_ada):
    depth, d, n = w_ada.shape
    tn = 1024
    return pl.pallas_call(
        _adaln_kernel,
        grid=(depth, n // tn),
        in_specs=[
            pl.BlockSpec((COND_ROWS, d), lambda l, j: (0, 0)),
            pl.BlockSpec((1, d, tn), lambda l, j: (l, 0, j)),
            pl.BlockSpec((1, 1, tn), lambda l, j: (l, 0, j)),
        ],
        out_specs=pl.BlockSpec((1, COND_ROWS, tn), lambda l, j: (l, 0, j)),
        out_shape=jax.ShapeDtypeStruct((depth, COND_ROWS, n), F32),
        compiler_params=_params("parallel", "parallel"),
        name="adaln",
    )(cond, w_ada, b_ada.reshape(depth, 1, n))


def _mod_spec(layer, d, chunk, row):
    return pl.BlockSpec((None, 1, 1, d), lambda b, i, j: (layer, row(b), 0, chunk))


def _norm_mod_matmul_kernel(x_ref, g_ref, sh_ref, sc_ref, w_ref, o_ref, h_ref):
    @pl.when(pl.program_id(2) == 0)
    def _():
        x = x_ref[0]
        ms = jnp.mean(x * x, axis=-1, keepdims=True)
        gain = g_ref[...] * (1.0 + sc_ref[0])
        h_ref[...] = (x * lax.rsqrt(ms + EPS) * gain + sh_ref[0]).astype(BF16)

    o_ref[0] = jnp.dot(h_ref[...], w_ref[...], preferred_element_type=F32).astype(o_ref.dtype)


def _norm_mod_matmul(x, g, mod, layer, row, shift_chunk, scale_chunk, w, tn, name):
    b, l, d = x.shape
    n = w.shape[-1]
    tm = min(l, 1024)
    return pl.pallas_call(
        _norm_mod_matmul_kernel,
        grid=(b, l // tm, n // tn),
        in_specs=[
            pl.BlockSpec((1, tm, d), lambda b, i, j: (b, i, 0)),
            pl.BlockSpec((1, d), lambda b, i, j: (0, 0)),
            _mod_spec(layer, d, shift_chunk, row),
            _mod_spec(layer, d, scale_chunk, row),
            pl.BlockSpec((None, d, tn), lambda b, i, j: (layer, 0, j)),
        ],
        out_specs=pl.BlockSpec((1, tm, tn), lambda b, i, j: (b, i, j)),
        out_shape=jax.ShapeDtypeStruct((b, l, n), BF16),
        scratch_shapes=[pltpu.VMEM((tm, d), BF16)],
        compiler_params=_params("parallel", "parallel", "arbitrary"),
        name=name,
    )(x, g.reshape(1, d), mod, mod, w)


def _matmul_residual_kernel(*refs, n_a):
    a_refs = refs[:n_a]
    w_ref, res_ref, gate_ref, o_ref = refs[n_a:]
    acc = None
    off = 0
    for a_ref in a_refs:
        k = a_ref.shape[-1]
        part = jnp.dot(a_ref[0], w_ref[off:off + k, :], preferred_element_type=F32)
        acc = part if acc is None else acc + part
        off += k
    o_ref[0] = res_ref[0] + gate_ref[0] * acc


def _matmul_residual(a_list, w, res, mod, layer, row, gate_chunk, name):
    b, l, n = res.shape
    k_total = w.shape[1]
    tm = min(l, 512)
    tn = n
    in_specs = [pl.BlockSpec((1, tm, a.shape[-1]), lambda b, i, j: (b, i, 0)) for a in a_list]
    in_specs += [
        pl.BlockSpec((None, k_total, tn), lambda b, i, j: (layer, 0, j)),
        pl.BlockSpec((1, tm, tn), lambda b, i, j: (b, i, j)),
        pl.BlockSpec((None, 1, 1, tn), lambda b, i, j: (layer, row(b), 0, gate_chunk * (n // tn) + j)),
    ]
    return pl.pallas_call(
        functools.partial(_matmul_residual_kernel, n_a=len(a_list)),
        grid=(b, l // tm, n // tn),
        in_specs=in_specs,
        out_specs=pl.BlockSpec((1, tm, tn), lambda b, i, j: (b, i, j)),
        out_shape=jax.ShapeDtypeStruct((b, l, n), F32),
        compiler_params=_params("parallel", "parallel", "arbitrary"),
        name=name,
    )(*a_list, w, res, mod)


def _rope(t, cos, sin):
    lane = lax.broadcasted_iota(jnp.int32, t.shape, 1)
    partner = jnp.where((lane & 32) == 0, pltpu.roll(t, LANES - 32, 1), pltpu.roll(t, 32, 1))
    return t * cos + partner * sin


def _retention_kernel(lg_ref, ql_ref, kl_ref, vl_ref, gl_ref, qc_ref, kc_ref, vc_ref, gc_ref, cos_ref, sin_ref,
                      gn_ref, *rest, with_ctx, n_lat, n_ctx):
    if with_ctx:
        ol_ref, oc_ref, acc_ref, qr_ref, kr_ref, accc_ref = rest
    else:
        ol_ref, acc_ref, qr_ref, kr_ref = rest
    c = RET_CHUNK
    scale = RET_DK ** -0.5
    h = pl.program_id(1)
    lgf = lg_ref[0, h]
    lgb = lg_ref[1, h]
    ii = lax.broadcasted_iota(jnp.int32, (c, c), 0).astype(F32)
    jj = lax.broadcasted_iota(jnp.int32, (c, c), 1).astype(F32)
    diff = ii - jj
    decay = (jnp.where(diff >= 0, jnp.exp(jnp.maximum(diff, 0.0) * lgf), 0.0)
             + jnp.where(diff <= 0, jnp.exp(jnp.maximum(-diff, 0.0) * lgb), 0.0))
    ic = lax.broadcasted_iota(jnp.int32, (c, 1), 0).astype(F32)
    xi_f = jnp.exp((ic + 1.0) * lgf)
    zeta_f = jnp.exp((c - 1.0 - ic) * lgf)
    xi_b = jnp.exp((c - ic) * lgb)
    zeta_b = jnp.exp(ic * lgb)
    g_f = jnp.exp(jnp.full((1, RET_DV), float(c), F32) * lgf)
    g_b = jnp.exp(jnp.full((1, RET_DV), float(c), F32) * lgb)
    gn = gn_ref[...]

    def intra_and_inter(qb, kb, v, xi, st):
        s = lax.dot_general(qb, kb, (((1,), (1,)), ((), ())), preferred_element_type=F32)
        p = (s * decay).astype(BF16)
        inter = jnp.dot(qb, st.astype(BF16), preferred_element_type=F32) * xi
        return jnp.dot(p, v, preferred_element_type=F32) + inter

    def next_state(st, k_f32, v, zeta, g):
        kz = (k_f32 * zeta).T.astype(BF16)
        return st * g + jnp.dot(kz, v, preferred_element_type=F32)

    def finish(total, gate):
        mu = jnp.mean(total, axis=-1, keepdims=True)
        d = total - mu
        var = jnp.mean(d * d, axis=-1, keepdims=True)
        y = d * lax.rsqrt(var + EPS) * gn
        return (y * (gate * _sigmoid(gate))).astype(BF16)

    st = jnp.zeros((RET_DK, RET_DV), F32)
    for j in range(n_ctx):
        rows = slice(j * c, (j + 1) * c)
        q = qc_ref[0, rows, :].astype(F32) * scale
        k = kc_ref[0, rows, :].astype(F32)
        v = vc_ref[0, rows, :]
        if with_ctx:
            accc_ref[rows, :] = intra_and_inter(q.astype(BF16), k.astype(BF16), v, xi_f, st)
        st = next_state(st, k, v, zeta_f, g_f)

    nt = (((1,), (1,)), ((), ()))
    group = RET_UNROLL
    assert n_lat % group == 0

    def fwd_body(jg, st):
        rows, qb, kb, v, kzv = [], [], [], [], []
        for u in range(group):
            r = pl.ds(pl.multiple_of((jg * group + u) * c, c), c)
            cos = cos_ref[r, :]
            sin = sin_ref[r, :]
            q = _rope(ql_ref[0, r, :].astype(F32), cos, sin) * scale
            k = _rope(kl_ref[0, r, :].astype(F32), cos, sin)
            rows.append(r)
            qb.append(q.astype(BF16))
            kb.append(k.astype(BF16))
            v.append(vl_ref[0, r, :])
            qr_ref[r, :] = qb[u]
            kr_ref[r, :] = kb[u]
            kzv.append(jnp.dot((k * zeta_f).T.astype(BF16), v[u], preferred_element_type=F32))
        s = [lax.dot_general(qb[u], kb[u], nt, preferred_element_type=F32) for u in range(group)]
        states = []
        for u in range(group):
            states.append(st.astype(BF16))
            st = st * g_f + kzv[u]
        p = [(s[u] * decay).astype(BF16) for u in range(group)]
        for u in range(group):
            inter = jnp.dot(qb[u], states[u], preferred_element_type=F32) * xi_f
            acc_ref[rows[u], :] = jnp.dot(p[u], v[u], preferred_element_type=F32) + inter
        return st

    lax.fori_loop(0, n_lat // group, fwd_body, st)

    st = jnp.zeros((RET_DK, RET_DV), F32)
    for j in reversed(range(n_ctx)):
        rows = slice(j * c, (j + 1) * c)
        k = kc_ref[0, rows, :].astype(F32)
        v = vc_ref[0, rows, :]
        if with_ctx:
            qb = (qc_ref[0, rows, :].astype(F32) * scale).astype(BF16)
            inter = jnp.dot(qb, st.astype(BF16), preferred_element_type=F32) * xi_b
            oc_ref[0, rows, :] = finish(accc_ref[rows, :] + inter, gc_ref[0, rows, :].astype(F32))
        st = next_state(st, k, v, zeta_b, g_b)

    def bwd_body(jg, st):
        rows, kzv = [], []
        for u in range(group):
            r = pl.ds(pl.multiple_of((n_lat - 1 - (jg * group + u)) * c, c), c)
            rows.append(r)
            kz = (kr_ref[r, :].astype(F32) * zeta_b).T.astype(BF16)
            kzv.append(jnp.dot(kz, vl_ref[0, r, :], preferred_element_type=F32))
        states = []
        for u in range(group):
            states.append(st.astype(BF16))
            st = st * g_b + kzv[u]
        inter = [jnp.dot(qr_ref[rows[u], :], states[u], preferred_element_type=F32) * xi_b for u in range(group)]
        for u in range(group):
            ol_ref[0, rows[u], :] = finish(acc_ref[rows[u], :] + inter[u], gl_ref[0, rows[u], :].astype(F32))
        return st

    lax.fori_loop(0, n_lat // group, bwd_body, st)


def _retention(p_l, p_c, log_gamma, gn_g, cos, sin, with_ctx):
    b, l, _ = p_l.shape
    lc = p_c.shape[1]
    c = RET_CHUNK

    def col(width, off, length):
        return pl.BlockSpec((1, length, width), lambda b, h: (b, 0, off // width + h))

    in_specs = [
        pl.BlockSpec(memory_space=pltpu.SMEM),
        col(RET_DK, OFF_RQ, l), col(RET_DK, OFF_RK, l), col(RET_DV, OFF_RV, l), col(RET_DV, OFF_RG, l),
        col(RET_DK, OFF_RQ, lc), col(RET_DK, OFF_RK, lc), col(RET_DV, OFF_RV, lc), col(RET_DV, OFF_RG, lc),
        pl.BlockSpec((l, RET_DK), lambda b, h: (0, 0)),
        pl.BlockSpec((l, RET_DK), lambda b, h: (0, 0)),
        pl.BlockSpec((1, RET_DV), lambda b, h: (0, h)),
    ]
    out_specs = [pl.BlockSpec((1, l, RET_DV), lambda b, h: (b, 0, h))]
    out_shape = [jax.ShapeDtypeStruct((b, l, RET_W), BF16)]
    scratch = [pltpu.VMEM((l, RET_DV), F32), pltpu.VMEM((l, RET_DK), BF16), pltpu.VMEM((l, RET_DK), BF16)]
    if with_ctx:
        out_specs.append(pl.BlockSpec((1, lc, RET_DV), lambda b, h: (b, 0, h)))
        out_shape.append(jax.ShapeDtypeStruct((b, lc, RET_W), BF16))
        scratch.append(pltpu.VMEM((lc, RET_DV), F32))
    outs = pl.pallas_call(
        functools.partial(_retention_kernel, with_ctx=with_ctx, n_lat=l // c, n_ctx=lc // c),
        grid=(b, RET_HEADS),
        in_specs=in_specs,
        out_specs=out_specs,
        out_shape=out_shape,
        scratch_shapes=scratch,
        compiler_params=_params("parallel", "parallel"),
        name="retention_ctx" if with_ctx else "retention",
    )(log_gamma, p_l, p_l, p_l, p_l, p_c, p_c, p_c, p_c, cos, sin, gn_g.reshape(1, RET_W))
    return (outs[0], outs[1]) if with_ctx else (outs[0], None)


def _conv_kernel(a_ref, b_ref, dww_ref, dwb_ref, lng_ref, lnb_ref, pw_ref, o_ref, upad_ref, act_ref, *, l):
    pad = CONV_PAD
    upad_ref[0:pad, :] = jnp.zeros((pad, CONV_W), F32)
    upad_ref[pad + l:pad + l + pad, :] = jnp.zeros((pad, CONV_W), F32)
    tg = min(l, 256)

    def glu_body(t, carry):
        r0 = pl.multiple_of(t * tg, tg)
        a = a_ref[0, pl.ds(r0, tg), :].astype(F32)
        b = b_ref[0, pl.ds(r0, tg), :].astype(F32)
        upad_ref[pl.ds(pad + r0, tg), :] = a * _sigmoid(b)
        return carry

    lax.fori_loop(0, l // tg, glu_body, 0)

    tile = CONV_TILE
    halo = 2 * pad
    lng = lng_ref[...]
    lnb = lnb_ref[...]

    def conv_body(t, carry):
        r0 = pl.multiple_of(t * tile, tile)
        ys = []
        for cb in range(CONV_W // LANES):
            cols = slice(cb * LANES, (cb + 1) * LANES)
            win = upad_ref[pl.ds(r0, tile + halo), cols]
            acc = jnp.zeros((tile, LANES), F32) + dwb_ref[:, cols]
            for r in range(SUBLANES):
                rolled = win if r == 0 else pltpu.roll(win, tile + halo - r, 0)
                for a in range(halo // SUBLANES):
                    k = SUBLANES * a + r - 1
                    if 0 <= k < CONV_K:
                        acc = acc + rolled[SUBLANES * a:SUBLANES * a + tile] * dww_ref[k:k + 1, cols]
            ys.append(acc)
        y = jnp.concatenate(ys, axis=-1)
        mu = jnp.mean(y, axis=-1, keepdims=True)
        d = y - mu
        var = jnp.mean(d * d, axis=-1, keepdims=True)
        u = d * lax.rsqrt(var + EPS) * lng + lnb
        act_ref[pl.ds(r0, tile), :] = (u * _sigmoid(u)).astype(BF16)
        return carry

    lax.fori_loop(0, l // tile, conv_body, 0)

    tmm = min(l, 512)

    def mm_body(t, carry):
        r0 = pl.multiple_of(t * tmm, tmm)
        o_ref[0, pl.ds(r0, tmm), :] = jnp.dot(act_ref[pl.ds(r0, tmm), :], pw_ref[...],
                                              preferred_element_type=F32).astype(BF16)
        return carry

    lax.fori_loop(0, l // tmm, mm_body, 0)


def _conv_module(p, dw_w, dw_b, ln_g, ln_b, pw):
    b, l, _ = p.shape
    vec = pl.BlockSpec((1, CONV_W), lambda b: (0, 0))
    return pl.pallas_call(
        functools.partial(_conv_kernel, l=l),
        grid=(b,),
        in_specs=[
            pl.BlockSpec((1, l, CONV_W), lambda b: (b, 0, OFF_CA // CONV_W)),
            pl.BlockSpec((1, l, CONV_W), lambda b: (b, 0, OFF_CB // CONV_W)),
            pl.BlockSpec((CONV_K, CONV_W), lambda b: (0, 0)),
            vec, vec, vec,
            pl.BlockSpec((CONV_W, CONV_W), lambda b: (0, 0)),
        ],
        out_specs=pl.BlockSpec((1, l, CONV_W), lambda b: (b, 0, 0)),
        out_shape=jax.ShapeDtypeStruct((b, l, CONV_W), BF16),
        scratch_shapes=[pltpu.VMEM((l + 2 * CONV_PAD, CONV_W), F32),
                        pltpu.VMEM((l, CONV_W), BF16)],
        compiler_params=_params("parallel"),
        name="conv_module",
    )(p, p, dw_w, dw_b.reshape(1, CONV_W), ln_g.reshape(1, CONV_W), ln_b.reshape(1, CONV_W), pw)


NA_RPB_ROWS = 2 * NA_ROWS - 1
NA_RPB_COLS = 2 * NA_COLS - 1
NA_PAIRS = NA_RPB_ROWS - 1
NA_QROWS = 4
NA_UNION = NA_ROWS + NA_QROWS
NA_GROUP = 2


def _na_kernel(rpb_ref, q_ref, k_ref, v_ref, kc_ref, vc_ref, o_ref, pairs_ref, *, rows_n, kh):
    scale = NA_DH ** -0.5
    h = pl.program_id(0)
    nt = (((1,), (1,)), ((), ()))

    @pl.when(pl.program_id(1) == 0)
    def _build_bias():
        lane = lax.broadcasted_iota(jnp.int32, (GRID_W, 2 * GRID_W), 1)
        qcol = lax.broadcasted_iota(jnp.int32, (GRID_W, 2 * GRID_W), 0)
        w = lane & (GRID_W - 1)
        col_off = jnp.clip(w - qcol + (NA_COLS - 1), 0, NA_RPB_COLS - 1)
        col_start = jnp.clip(qcol - NA_COLS // 2, 0, GRID_W - NA_COLS)
        col_in = (w >= col_start) & (w < col_start + NA_COLS)
        left = lane < GRID_W

        def pair_body(k, carry):
            row = h * NA_RPB_ROWS + k
            acc = jnp.zeros((GRID_W, 2 * GRID_W), F32)
            for d in range(NA_RPB_COLS):
                val = jnp.where(left, rpb_ref[row, d], rpb_ref[row + 1, d])
                acc = jnp.where(col_off == d, val, acc)
            pairs_ref[k] = jnp.where(col_in, acc, MASK_VALUE)
            return carry

        lax.fori_loop(0, NA_PAIRS, pair_body, 0)

    kc = kc_ref[0]
    vc = vc_ref[0]

    lane = lax.broadcasted_iota(jnp.int32, (GRID_W, 2 * GRID_W), 1)
    left = lane < GRID_W
    nq = NA_QROWS * GRID_W
    nk = NA_UNION * GRID_W

    def scores(g):
        r0 = g * NA_QROWS
        u0 = jnp.clip(r0 - kh // 2, 0, rows_n - NA_UNION)
        q0 = pl.multiple_of(r0 * GRID_W, nq)
        k0 = pl.multiple_of(u0 * GRID_W, GRID_W)
        q = q_ref[0, pl.ds(q0, nq), :]
        kw = k_ref[0, pl.ds(k0, nk), :]
        vw = v_ref[0, pl.ds(k0, nk), :]
        bias_rows = []
        for rq in range(NA_QROWS):
            r = r0 + rq
            s0 = jnp.clip(r - kh // 2, 0, rows_n - kh)
            tiles = []
            for m in range(NA_UNION // 2):
                kr = u0 + 2 * m
                ok_l = (kr >= s0) & (kr < s0 + kh)
                ok_r = (kr + 1 >= s0) & (kr + 1 < s0 + kh)
                ok = jnp.where(left, ok_l.astype(jnp.int32), ok_r.astype(jnp.int32)) != 0
                pair = pairs_ref[jnp.clip(kr - r + (NA_ROWS - 1), 0, NA_PAIRS - 1)]
                tiles.append(jnp.where(ok, pair, MASK_VALUE))
            bias_rows.append(jnp.concatenate(tiles, axis=-1))
        bias = jnp.concatenate(bias_rows, axis=0)
        s_lat = lax.dot_general(q, kw, nt, preferred_element_type=F32) * scale + bias
        s_ctx = lax.dot_general(q, kc, nt, preferred_element_type=F32) * scale
        return s_lat, s_ctx, vw, q0

    def weights(s_lat, s_ctx):
        m = jnp.maximum(jnp.max(s_lat, axis=-1, keepdims=True), jnp.max(s_ctx, axis=-1, keepdims=True))
        e_lat = jnp.exp(s_lat - m)
        e_ctx = jnp.exp(s_ctx - m)
        den = jnp.sum(e_lat, axis=-1, keepdims=True) + jnp.sum(e_ctx, axis=-1, keepdims=True)
        return e_lat.astype(BF16), e_ctx.astype(BF16), den

    def body(t, carry):
        sc = [scores(t * NA_GROUP + u) for u in range(NA_GROUP)]
        ws = [weights(s[0], s[1]) for s in sc]
        for (s_lat, s_ctx, vw, q0), (e_lat, e_ctx, den) in zip(sc, ws):
            o = jnp.dot(e_lat, vw, preferred_element_type=F32) + jnp.dot(e_ctx, vc, preferred_element_type=F32)
            o_ref[0, pl.ds(q0, nq), :] = (o / den).astype(BF16)
        return carry

    lax.fori_loop(0, rows_n // (NA_QROWS * NA_GROUP), body, 0)


def _na_latent(p_l, p_c, rpb):
    b, l, _ = p_l.shape
    lc = p_c.shape[1]
    rows_n = l // GRID_W
    kh = NA_ROWS
    assert rpb.shape == (NA_HEADS, NA_RPB_ROWS, NA_RPB_COLS)
    assert rows_n >= NA_UNION and rows_n % (NA_QROWS * NA_GROUP) == 0 and (rows_n - NA_UNION) % 2 == 0

    def col(off, length):
        return pl.BlockSpec((1, length, NA_DH), lambda h, b: (b, 0, off // NA_DH + h))

    return pl.pallas_call(
        functools.partial(_na_kernel, rows_n=rows_n, kh=kh),
        grid=(NA_HEADS, b),
        in_specs=[pl.BlockSpec(memory_space=pltpu.SMEM),
                  col(OFF_NQ, l), col(OFF_NK, l), col(OFF_NV, l), col(OFF_NK, lc), col(OFF_NV, lc)],
        out_specs=pl.BlockSpec((1, l, NA_DH), lambda h, b: (b, 0, h)),
        out_shape=jax.ShapeDtypeStruct((b, l, NA_W), BF16),
        scratch_shapes=[pltpu.VMEM((NA_PAIRS, GRID_W, 2 * GRID_W), F32)],
        compiler_params=_params("parallel", "arbitrary"),
        name="na_latent",
    )(rpb.astype(F32).reshape(NA_HEADS * NA_RPB_ROWS, NA_RPB_COLS), p_l, p_l, p_l, p_c, p_c)


def _na_ctx_kernel(q_ref, k_ref, v_ref, o_ref):
    scale = NA_DH ** -0.5
    s = lax.dot_general(q_ref[0], k_ref[0], (((1,), (1,)), ((), ())), preferred_element_type=F32) * scale
    e = jnp.exp(s - jnp.max(s, axis=-1, keepdims=True))
    o = jnp.dot(e.astype(BF16), v_ref[0], preferred_element_type=F32)
    o_ref[0] = (o / jnp.sum(e, axis=-1, keepdims=True)).astype(BF16)


def _na_context(p_c):
    b, lc, _ = p_c.shape

    def col(off):
        return pl.BlockSpec((1, lc, NA_DH), lambda b, h: (b, 0, off // NA_DH + h))

    return pl.pallas_call(
        _na_ctx_kernel,
        grid=(b, NA_HEADS),
        in_specs=[col(OFF_NQ), col(OFF_NK), col(OFF_NV)],
        out_specs=pl.BlockSpec((1, lc, NA_DH), lambda b, h: (b, 0, h)),
        out_shape=jax.ShapeDtypeStruct((b, lc, NA_W), BF16),
        compiler_params=_params("parallel", "parallel"),
        name="na_context",
    )(p_c, p_c, p_c)


FFN_HALO = 16
FFN_SLICE = 256
def _ffn_kernel(x_ref, xp_ref, xn_ref, g_ref, sh_ref, sc_ref, gate_ref, upv_ref, upg_ref, wv_ref, wg_ref, bv_ref,
                bg_ref, down_ref, fg_ref, o_ref, h_ref, *, final_norm):
    i = pl.program_id(1)
    j = pl.program_id(2)
    tm = x_ref.shape[1]

    def norm_mod(x):
        ms = jnp.mean(x * x, axis=-1, keepdims=True)
        gain = g_ref[...] * (1.0 + sc_ref[0])
        return x * lax.rsqrt(ms + EPS) * gain + sh_ref[0]

    @pl.when(j == 0)
    def _():
        h_ref[0:tm, :] = norm_mod(x_ref[0]).astype(BF16)
        hp = jnp.where(i == 0, 0.0, norm_mod(xp_ref[0]))
        hn = jnp.where(i == pl.num_programs(1) - 1, 0.0, norm_mod(xn_ref[0]))
        h_ref[tm:tm + FFN_HALO, :] = jnp.concatenate([hp, hn], axis=0).astype(BF16)
        o_ref[0] = jnp.zeros(o_ref.shape[1:], F32)

    h = h_ref[...]
    edge = lax.broadcasted_iota(jnp.int32, (SUBLANES, 1), 0)

    def conv3(u, w_ref, b_ref, c):
        x = u[0:tm]
        prev_row = u[tm + SUBLANES - 1:tm + SUBLANES]
        next_row = u[tm + SUBLANES:tm + SUBLANES + 1]
        x_prev = pltpu.roll(x, 1, 0)
        x_next = pltpu.roll(x, tm - 1, 0)
        x_prev = jnp.concatenate([jnp.where(edge == 0, prev_row, x_prev[0:SUBLANES]), x_prev[SUBLANES:]], axis=0)
        x_next = jnp.concatenate([x_next[:tm - SUBLANES],
                                  jnp.where(edge == SUBLANES - 1, next_row, x_next[tm - SUBLANES:])], axis=0)
        return x_prev * w_ref[0:1, c] + x * w_ref[1:2, c] + x_next * w_ref[2:3, c] + b_ref[:, c]

    tf = upv_ref.shape[1]
    width = min(tf, FFN_SLICE)
    slices = [slice(s * width, (s + 1) * width) for s in range(tf // width)]
    us = []
    for c in slices:
        us.append((jnp.dot(h, upv_ref[:, c], preferred_element_type=F32),
                   jnp.dot(h, upg_ref[:, c], preferred_element_type=F32)))
    for (u_val, u_gate), c in zip(us, slices):
        val = conv3(u_val, wv_ref, bv_ref, c)
        gate = conv3(u_gate, wg_ref, bg_ref, c)
        act = (gate * _sigmoid(gate) * val).astype(BF16)
        o_ref[0] += jnp.dot(act, down_ref[c, :], preferred_element_type=F32)

    @pl.when(j == pl.num_programs(2) - 1)
    def _():
        y = x_ref[0] + gate_ref[0] * o_ref[0]
        if final_norm:
            ms = jnp.mean(y * y, axis=-1, keepdims=True)
            y = y * lax.rsqrt(ms + EPS) * fg_ref[...]
        o_ref[0] = y


def _ffn(x, g, mod, layer, row, up, dw_w, dw_b, down, final_g, final_norm, name):
    b, l, d = x.shape
    f = down.shape[1]
    tm = min(l, 512)
    tf = min(f, 512)
    nf = f // tf
    rb = tm // SUBLANES

    def mspec(chunk):
        return pl.BlockSpec((None, 1, 1, d), lambda b, i, j: (layer, row(b), 0, chunk))

    def wspec(rows, off):
        return pl.BlockSpec((rows, tf), lambda b, i, j: (0, off + j))

    in_specs = [
        pl.BlockSpec((1, tm, d), lambda b, i, j: (b, i, 0)),
        pl.BlockSpec((1, SUBLANES, d), lambda b, i, j: (b, jnp.maximum(i * rb - 1, 0), 0)),
        pl.BlockSpec((1, SUBLANES, d), lambda b, i, j: (b, jnp.minimum((i + 1) * rb, l // SUBLANES - 1), 0)),
        pl.BlockSpec((1, d), lambda b, i, j: (0, 0)),
        mspec(3), mspec(4), mspec(5),
        pl.BlockSpec((None, d, tf), lambda b, i, j: (layer, 0, j)),
        pl.BlockSpec((None, d, tf), lambda b, i, j: (layer, 0, nf + j)),
        wspec(FFN_K, 0), wspec(FFN_K, nf), wspec(1, 0), wspec(1, nf),
        pl.BlockSpec((None, tf, d), lambda b, i, j: (layer, j, 0)),
        pl.BlockSpec((1, d), lambda b, i, j: (0, 0)),
    ]
    return pl.pallas_call(
        functools.partial(_ffn_kernel, final_norm=final_norm),
        grid=(b, l // tm, nf),
        in_specs=in_specs,
        out_specs=pl.BlockSpec((1, tm, d), lambda b, i, j: (b, i, 0)),
        out_shape=jax.ShapeDtypeStruct((b, l, d), F32),
        scratch_shapes=[pltpu.VMEM((tm + FFN_HALO, d), BF16)],
        compiler_params=_params("parallel", "parallel", "arbitrary"),
        name=name,
    )(x, x, x, g.reshape(1, d), mod, mod, mod, up, up, dw_w, dw_w, dw_b.reshape(1, 2 * f), dw_b.reshape(1, 2 * f),
      down, final_g.reshape(1, d))


def _rope_tables(l):
    nf = RET_DK // 4
    pos = np.arange(l)
    inv = ROPE_BASE ** (-jnp.arange(nf, dtype=F32) / nf)
    ang_r = jnp.asarray(pos // GRID_W, F32)[:, None] * inv[None, :]
    ang_c = jnp.asarray(pos % GRID_W, F32)[:, None] * inv[None, :]
    cos = jnp.concatenate([jnp.cos(ang_r), jnp.cos(ang_r), jnp.cos(ang_c), jnp.cos(ang_c)], axis=-1)
    sin = jnp.concatenate([-jnp.sin(ang_r), jnp.sin(ang_r), -jnp.sin(ang_c), jnp.sin(ang_c)], axis=-1)
    return cos, sin


def kernel(x, c, ctx, c_ctx, w_ada, b_ada, norm1_g, w_in, ret_decay, ret_gn_g, conv_dw_w, conv_dw_b, conv_ln_g,
           conv_ln_b, conv_pw, na_rpb, w_out, norm2_g, ffn_up, ffn_dw_w, ffn_dw_b, ffn_down, final_g):
    depth = w_ada.shape[0]
    bsz, l, d = x.shape

    cond = jnp.zeros((COND_ROWS, d), F32).at[:bsz].set(c).at[bsz].set(c_ctx)
    mod = _adaln(cond, w_ada, b_ada).reshape(depth, COND_ROWS, 1, 6 * d)
    lat_row = lambda b: b
    ctx_row = lambda b: bsz
    cos, sin = _rope_tables(l)
    log_gamma = jax.nn.log_sigmoid(ret_decay.astype(F32))
    w_in_b = w_in.astype(BF16)
    w_out_b = w_out.astype(BF16)
    up_b = ffn_up.astype(BF16)
    down_b = ffn_down.astype(BF16)
    tn_in = w_in.shape[-1] // 4

    h_ctx = ctx
    for layer in range(depth):
        last = layer == depth - 1
        pw_b = conv_pw[layer].astype(BF16)

        p_l = _norm_mod_matmul(x, norm1_g[layer], mod, layer, lat_row, 0, 1, w_in_b, tn_in, "in_proj")
        p_c = _norm_mod_matmul(h_ctx, norm1_g[layer], mod, layer, ctx_row, 0, 1, w_in_b, tn_in, "in_proj_ctx")

        ret_l, ret_c = _retention(p_l, p_c, log_gamma[layer], ret_gn_g[layer], cos, sin, with_ctx=not last)
        conv_l = _conv_module(p_l, conv_dw_w[layer], conv_dw_b[layer], conv_ln_g[layer], conv_ln_b[layer], pw_b)
        na_l = _na_latent(p_l, p_c, na_rpb[layer])
        x = _matmul_residual([ret_l, conv_l, na_l], w_out_b, x, mod, layer, lat_row, 2, "out_proj")

        x = _ffn(x, norm2_g[layer], mod, layer, lat_row, up_b, ffn_dw_w[layer], ffn_dw_b[layer], down_b, final_g,
                 last, "ffn")

        if not last:
            conv_c = _conv_module(p_c, conv_dw_w[layer], conv_dw_b[layer], conv_ln_g[layer], conv_ln_b[layer], pw_b)
            na_c = _na_context(p_c)
            h_ctx = _matmul_residual([ret_c, conv_c, na_c], w_out_b, h_ctx, mod, layer, ctx_row, 2, "out_proj_ctx")
            h_ctx = _ffn(h_ctx, norm2_g[layer], mod, layer, ctx_row, up_b, ffn_dw_w[layer], ffn_dw_b[layer], down_b,
                         final_g, False, "ffn_ctx")

    return x
```

```python
import functools

import numpy as np
import jax
import jax.numpy as jnp
from jax import lax
from jax.experimental import pallas as pl
from jax.experimental.pallas import tpu as pltpu

F32 = jnp.float32
BF16 = jnp.bfloat16

GRID_W = 64
RET_HEADS = 4
RET_DK = 128
RET_DV = 256
RET_CHUNK = 128
RET_UNROLL = 8
CONV_W = 512
CONV_K = 31
NA_HEADS = 4
NA_DH = 128
NA_ROWS = 8
NA_COLS = 16
FFN_K = 3
ROPE_BASE = 10000.0
EPS = 1e-6

RET_QK_W = RET_HEADS * RET_DK
RET_W = RET_HEADS * RET_DV
NA_W = NA_HEADS * NA_DH
OFF_RQ = 0
OFF_RK = OFF_RQ + RET_QK_W
OFF_RV = OFF_RK + RET_QK_W
OFF_RG = OFF_RV + RET_W
OFF_CA = OFF_RG + RET_W
OFF_CB = OFF_CA + CONV_W
OFF_NQ = OFF_CB + CONV_W
OFF_NK = OFF_NQ + NA_W
OFF_NV = OFF_NK + NA_W

V7X_VMEM_BYTES = 64 * 1024 * 1024
VMEM_LIMIT = V7X_VMEM_BYTES - 8 * 1024 * 1024
SUBLANES = 8
LANES = 128
COND_ROWS = 8
MASK_VALUE = -1e30
CONV_PAD = 16
CONV_TILE = 64


def _sigmoid(x):
    return 1.0 / (1.0 + jnp.exp(-x))


def _params(*sem):
    return pltpu.CompilerParams(dimension_semantics=sem, vmem_limit_bytes=VMEM_LIMIT)


def _adaln_kernel(c_ref, w_ref, b_ref, o_ref):
    c = c_ref[...]
    s = (c * _sigmoid(c)).astype(BF16)
    o_ref[0] = jnp.dot(s, w_ref[0].astype(BF16), preferred_element_type=F32) + b_ref[0]


def _adaln(cond, w_ada, b_ada):
    depth, d, n = w_ada.shape
    tn = 1024
    return pl.pallas_call(
        _adaln_kernel,
        grid=(depth, n // tn),
        in_specs=[
            pl.BlockSpec((COND_ROWS, d), lambda l, j: (0, 0)),
            pl.BlockSpec((1, d, tn), lambda l, j: (l, 0, j)),
            pl.BlockSpec((1, 1, tn), lambda l, j: (l, 0, j)),
        ],
        out_specs=pl.BlockSpec((1, COND_ROWS, tn), lambda l, j: (l, 0, j)),
        out_shape=jax.ShapeDtypeStruct((depth, COND_ROWS, n), F32),
        compiler_params=_params("parallel", "parallel"),
        name="adaln",
    )(cond, w_ada, b_ada.reshape(depth, 1, n))


def _mod_spec(layer, d, chunk, row):
    return pl.BlockSpec((None, 1, 1, d), lambda b, i, j: (layer, row(b), 0, chunk))


def _norm_mod_matmul_kernel(x_ref, g_ref, sh_ref, sc_ref, w_ref, o_ref, h_ref):
    @pl.when(pl.program_id(2) == 0)
    def _():
        x = x_ref[0]
        ms = jnp.mean(x * x, axis=-1, keepdims=True)
        gain = g_ref[...] * (1.0 + sc_ref[0])
        h_ref[...] = (x * lax.rsqrt(ms + EPS) * gain + sh_ref[0]).astype(BF16)

    o_ref[0] = jnp.dot(h_ref[...], w_ref[...], preferred_element_type=F32).astype(o_ref.dtype)


def _norm_mod_matmul(x, g, mod, layer, row, shift_chunk, scale_chunk, w, tn, name):
    b, l, d = x.shape
    n = w.shape[-1]
    tm = min(l, 1024)
    return pl.pallas_call(
        _norm_mod_matmul_kernel,
        grid=(b, l // tm, n // tn),
        in_specs=[
            pl.BlockSpec((1, tm, d), lambda b, i, j: (b, i, 0)),
            pl.BlockSpec((1, d), lambda b, i, j: (0, 0)),
            _mod_spec(layer, d, shift_chunk, row),
            _mod_spec(layer, d, scale_chunk, row),
            pl.BlockSpec((None, d, tn), lambda b, i, j: (layer, 0, j)),
        ],
        out_specs=pl.BlockSpec((1, tm, tn), lambda b, i, j: (b, i, j)),
        out_shape=jax.ShapeDtypeStruct((b, l, n), BF16),
        scratch_shapes=[pltpu.VMEM((tm, d), BF16)],
        compiler_params=_params("parallel", "parallel", "arbitrary"),
        name=name,
    )(x, g.reshape(1, d), mod, mod, w)


def _matmul_residual_kernel(*refs, n_a):
    a_refs = refs[:n_a]
    w_ref, res_ref, gate_ref, o_ref = refs[n_a:]
    acc = None
    off = 0
    for a_ref in a_refs:
        k = a_ref.shape[-1]
        part = jnp.dot(a_ref[0], w_ref[off:off + k, :], preferred_element_type=F32)
        acc = part if acc is None else acc + part
        off += k
    o_ref[0] = res_ref[0] + gate_ref[0] * acc


def _matmul_residual(a_list, w, res, mod, layer, row, gate_chunk, name):
    b, l, n = res.shape
    k_total = w.shape[1]
    tm = min(l, 512)
    tn = n
    in_specs = [pl.BlockSpec((1, tm, a.shape[-1]), lambda b, i, j: (b, i, 0)) for a in a_list]
    in_specs += [
        pl.BlockSpec((None, k_total, tn), lambda b, i, j: (layer, 0, j)),
        pl.BlockSpec((1, tm, tn), lambda b, i, j: (b, i, j)),
        pl.BlockSpec((None, 1, 1, tn), lambda b, i, j: (layer, row(b), 0, gate_chunk * (n // tn) + j)),
    ]
    return pl.pallas_call(
        functools.partial(_matmul_residual_kernel, n_a=len(a_list)),
        grid=(b, l // tm, n // tn),
        in_specs=in_specs,
        out_specs=pl.BlockSpec((1, tm, tn), lambda b, i, j: (b, i, j)),
        out_shape=jax.ShapeDtypeStruct((b, l, n), F32),
        compiler_params=_params("parallel", "parallel", "arbitrary"),
        name=name,
    )(*a_list, w, res, mod)


def _rope(t, cos, sin):
    lane = lax.broadcasted_iota(jnp.int32, t.shape, 1)
    partner = jnp.where((lane & 32) == 0, pltpu.roll(t, LANES - 32, 1), pltpu.roll(t, 32, 1))
    return t * cos + partner * sin


def _retention_kernel(lg_ref, ql_ref, kl_ref, vl_ref, gl_ref, qc_ref, kc_ref, vc_ref, gc_ref, cos_ref, sin_ref,
                      gn_ref, *rest, with_ctx, n_lat, n_ctx):
    if with_ctx:
        ol_ref, oc_ref, acc_ref, qr_ref, kr_ref, accc_ref = rest
    else:
        ol_ref, acc_ref, qr_ref, kr_ref = rest
    c = RET_CHUNK
    scale = RET_DK ** -0.5
    h = pl.program_id(1)
    lgf = lg_ref[0, h]
    lgb = lg_ref[1, h]
    ii = lax.broadcasted_iota(jnp.int32, (c, c), 0).astype(F32)
    jj = lax.broadcasted_iota(jnp.int32, (c, c), 1).astype(F32)
    diff = ii - jj
    decay = (jnp.where(diff >= 0, jnp.exp(jnp.maximum(diff, 0.0) * lgf), 0.0)
             + jnp.where(diff <= 0, jnp.exp(jnp.maximum(-diff, 0.0) * lgb), 0.0))
    ic = lax.broadcasted_iota(jnp.int32, (c, 1), 0).astype(F32)
    xi_f = jnp.exp((ic + 1.0) * lgf)
    zeta_f = jnp.exp((c - 1.0 - ic) * lgf)
    xi_b = jnp.exp((c - ic) * lgb)
    zeta_b = jnp.exp(ic * lgb)
    g_f = jnp.exp(jnp.full((1, RET_DV), float(c), F32) * lgf)
    g_b = jnp.exp(jnp.full((1, RET_DV), float(c), F32) * lgb)
    gn = gn_ref[...]

    def intra_and_inter(qb, kb, v, xi, st):
        s = lax.dot_general(qb, kb, (((1,), (1,)), ((), ())), preferred_element_type=F32)
        p = (s * decay).astype(BF16)
        inter = jnp.dot(qb, st.astype(BF16), preferred_element_type=F32) * xi
        return jnp.dot(p, v, preferred_element_type=F32) + inter

    def next_state(st, k_f32, v, zeta, g):
        kz = (k_f32 * zeta).T.astype(BF16)
        return st * g + jnp.dot(kz, v, preferred_element_type=F32)

    def finish(total, gate):
        mu = jnp.mean(total, axis=-1, keepdims=True)
        d = total - mu
        var = jnp.mean(d * d, axis=-1, keepdims=True)
        y = d * lax.rsqrt(var + EPS) * gn
        return (y * (gate * _sigmoid(gate))).astype(BF16)

    st = jnp.zeros((RET_DK, RET_DV), F32)
    for j in range(n_ctx):
        rows = slice(j * c, (j + 1) * c)
        q = qc_ref[0, rows, :].astype(F32) * scale
        k = kc_ref[0, rows, :].astype(F32)
        v = vc_ref[0, rows, :]
        if with_ctx:
            accc_ref[rows, :] = intra_and_inter(q.astype(BF16), k.astype(BF16), v, xi_f, st)
        st = next_state(st, k, v, zeta_f, g_f)

    nt = (((1,), (1,)), ((), ()))
    group = RET_UNROLL
    assert n_lat % group == 0

    def fwd_body(jg, st):
        rows, qb, kb, v, kzv = [], [], [], [], []
        for u in range(group):
            r = pl.ds(pl.multiple_of((jg * group + u) * c, c), c)
            cos = cos_ref[r, :]
            sin = sin_ref[r, :]
            q = _rope(ql_ref[0, r, :].astype(F32), cos, sin) * scale
            k = _rope(kl_ref[0, r, :].astype(F32), cos, sin)
            rows.append(r)
            qb.append(q.astype(BF16))
            kb.append(k.astype(BF16))
            v.append(vl_ref[0, r, :])
            qr_ref[r, :] = qb[u]
            kr_ref[r, :] = kb[u]
            kzv.append(jnp.dot((k * zeta_f).T.astype(BF16), v[u], preferred_element_type=F32))
        s = [lax.dot_general(qb[u], kb[u], nt, preferred_element_type=F32) for u in range(group)]
        states = []
        for u in range(group):
            states.append(st.astype(BF16))
            st = st * g_f + kzv[u]
        p = [(s[u] * decay).astype(BF16) for u in range(group)]
        for u in range(group):
            inter = jnp.dot(qb[u], states[u], preferred_element_type=F32) * xi_f
            acc_ref[rows[u], :] = jnp.dot(p[u], v[u], preferred_element_type=F32) + inter
        return st

    lax.fori_loop(0, n_lat // group, fwd_body, st)

    st = jnp.zeros((RET_DK, RET_DV), F32)
    for j in reversed(range(n_ctx)):
        rows = slice(j * c, (j + 1) * c)
        k = kc_ref[0, rows, :].astype(F32)
        v = vc_ref[0, rows, :]
        if with_ctx:
            qb = (qc_ref[0, rows, :].astype(F32) * scale).astype(BF16)
            inter = jnp.dot(qb, st.astype(BF16), preferred_element_type=F32) * xi_b
            oc_ref[0, rows, :] = finish(accc_ref[rows, :] + inter, gc_ref[0, rows, :].astype(F32))
        st = next_state(st, k, v, zeta_b, g_b)

    def bwd_body(jg, st):
        rows, kzv = [], []
        for u in range(group):
            r = pl.ds(pl.multiple_of((n_lat - 1 - (jg * group + u)) * c, c), c)
            rows.append(r)
            kz = (kr_ref[r, :].astype(F32) * zeta_b).T.astype(BF16)
            kzv.append(jnp.dot(kz, vl_ref[0, r, :], preferred_element_type=F32))
        states = []
        for u in range(group):
            states.append(st.astype(BF16))
            st = st * g_b + kzv[u]
        inter = [jnp.dot(qr_ref[rows[u], :], states[u], preferred_element_type=F32) * xi_b for u in range(group)]
        for u in range(group):
            ol_ref[0, rows[u], :] = finish(acc_ref[rows[u], :] + inter[u], gl_ref[0, rows[u], :].astype(F32))
        return st

    lax.fori_loop(0, n_lat // group, bwd_body, st)


def _retention(p_l, p_c, log_gamma, gn_g, cos, sin, with_ctx):
    b, l, _ = p_l.shape
    lc = p_c.shape[1]
    c = RET_CHUNK

    def col(width, off, length):
        return pl.BlockSpec((1, length, width), lambda b, h: (b, 0, off // width + h))

    in_specs = [
        pl.BlockSpec(memory_space=pltpu.SMEM),
        col(RET_DK, OFF_RQ, l), col(RET_DK, OFF_RK, l), col(RET_DV, OFF_RV, l), col(RET_DV, OFF_RG, l),
        col(RET_DK, OFF_RQ, lc), col(RET_DK, OFF_RK, lc), col(RET_DV, OFF_RV, lc), col(RET_DV, OFF_RG, lc),
        pl.BlockSpec((l, RET_DK), lambda b, h: (0, 0)),
        pl.BlockSpec((l, RET_DK), lambda b, h: (0, 0)),
        pl.BlockSpec((1, RET_DV), lambda b, h: (0, h)),
    ]
    out_specs = [pl.BlockSpec((1, l, RET_DV), lambda b, h: (b, 0, h))]
    out_shape = [jax.ShapeDtypeStruct((b, l, RET_W), BF16)]
    scratch = [pltpu.VMEM((l, RET_DV), F32), pltpu.VMEM((l, RET_DK), BF16), pltpu.VMEM((l, RET_DK), BF16)]
    if with_ctx:
        out_specs.append(pl.BlockSpec((1, lc, RET_DV), lambda b, h: (b, 0, h)))
        out_shape.append(jax.ShapeDtypeStruct((b, lc, RET_W), BF16))
        scratch.append(pltpu.VMEM((lc, RET_DV), F32))
    outs = pl.pallas_call(
        functools.partial(_retention_kernel, with_ctx=with_ctx, n_lat=l // c, n_ctx=lc // c),
        grid=(b, RET_HEADS),
        in_specs=in_specs,
        out_specs=out_specs,
        out_shape=out_shape,
        scratch_shapes=scratch,
        compiler_params=_params("parallel", "parallel"),
        name="retention_ctx" if with_ctx else "retention",
    )(log_gamma, p_l, p_l, p_l, p_l, p_c, p_c, p_c, p_c, cos, sin, gn_g.reshape(1, RET_W))
    return (outs[0], outs[1]) if with_ctx else (outs[0], None)


def _conv_kernel(a_ref, b_ref, dww_ref, dwb_ref, lng_ref, lnb_ref, pw_ref, o_ref, upad_ref, act_ref, *, l):
    pad = CONV_PAD
    upad_ref[0:pad, :] = jnp.zeros((pad, CONV_W), F32)
    upad_ref[pad + l:pad + l + pad, :] = jnp.zeros((pad, CONV_W), F32)
    tg = min(l, 256)

    def glu_body(t, carry):
        r0 = pl.multiple_of(t * tg, tg)
        a = a_ref[0, pl.ds(r0, tg), :].astype(F32)
        b = b_ref[0, pl.ds(r0, tg), :].astype(F32)
        upad_ref[pl.ds(pad + r0, tg), :] = a * _sigmoid(b)
        return carry

    lax.fori_loop(0, l // tg, glu_body, 0)

    tile = CONV_TILE
    halo = 2 * pad
    lng = lng_ref[...]
    lnb = lnb_ref[...]

    def conv_body(t, carry):
        r0 = pl.multiple_of(t * tile, tile)
        ys = []
        for cb in range(CONV_W // LANES):
            cols = slice(cb * LANES, (cb + 1) * LANES)
            win = upad_ref[pl.ds(r0, tile + halo), cols]
            acc = jnp.zeros((tile, LANES), F32) + dwb_ref[:, cols]
            for r in range(SUBLANES):
                rolled = win if r == 0 else pltpu.roll(win, tile + halo - r, 0)
                for a in range(halo // SUBLANES):
                    k = SUBLANES * a + r - 1
                    if 0 <= k < CONV_K:
                        acc = acc + rolled[SUBLANES * a:SUBLANES * a + tile] * dww_ref[k:k + 1, cols]
            ys.append(acc)
        y = jnp.concatenate(ys, axis=-1)
        mu = jnp.mean(y, axis=-1, keepdims=True)
        d = y - mu
        var = jnp.mean(d * d, axis=-1, keepdims=True)
        u = d * lax.rsqrt(var + EPS) * lng + lnb
        act_ref[pl.ds(r0, tile), :] = (u * _sigmoid(u)).astype(BF16)
        return carry

    lax.fori_loop(0, l // tile, conv_body, 0)

    tmm = min(l, 512)

    def mm_body(t, carry):
        r0 = pl.multiple_of(t * tmm, tmm)
        o_ref[0, pl.ds(r0, tmm), :] = jnp.dot(act_ref[pl.ds(r0, tmm), :], pw_ref[...],
                                              preferred_element_type=F32).astype(BF16)
        return carry

    lax.fori_loop(0, l // tmm, mm_body, 0)


def _conv_module(p, dw_w, dw_b, ln_g, ln_b, pw):
    b, l, _ = p.shape
    vec = pl.BlockSpec((1, CONV_W), lambda b: (0, 0))
    return pl.pallas_call(
        functools.partial(_conv_kernel, l=l),
        grid=(b,),
        in_specs=[
            pl.BlockSpec((1, l, CONV_W), lambda b: (b, 0, OFF_CA // CONV_W)),
            pl.BlockSpec((1, l, CONV_W), lambda b: (b, 0, OFF_CB // CONV_W)),
            pl.BlockSpec((CONV_K, CONV_W), lambda b: (0, 0)),
            vec, vec, vec,
            pl.BlockSpec((CONV_W, CONV_W), lambda b: (0, 0)),
        ],
        out_specs=pl.BlockSpec((1, l, CONV_W), lambda b: (b, 0, 0)),
        out_shape=jax.ShapeDtypeStruct((b, l, CONV_W), BF16),
        scratch_shapes=[pltpu.VMEM((l + 2 * CONV_PAD, CONV_W), F32),
                        pltpu.VMEM((l, CONV_W), BF16)],
        compiler_params=_params("parallel"),
        name="conv_module",
    )(p, p, dw_w, dw_b.reshape(1, CONV_W), ln_g.reshape(1, CONV_W), ln_b.reshape(1, CONV_W), pw)


NA_RPB_ROWS = 2 * NA_ROWS - 1
NA_RPB_COLS = 2 * NA_COLS - 1
NA_PAIRS = NA_RPB_ROWS - 1
NA_QROWS = 4
NA_UNION = NA_ROWS + NA_QROWS
NA_GROUP = 2


def _na_kernel(rpb_ref, q_ref, k_ref, v_ref, kc_ref, vc_ref, o_ref, pairs_ref, *, rows_n, kh):
    scale = NA_DH ** -0.5
    h = pl.program_id(0)
    nt = (((1,), (1,)), ((), ()))

    @pl.when(pl.program_id(1) == 0)
    def _build_bias():
        lane = lax.broadcasted_iota(jnp.int32, (GRID_W, 2 * GRID_W), 1)
        qcol = lax.broadcasted_iota(jnp.int32, (GRID_W, 2 * GRID_W), 0)
        w = lane & (GRID_W - 1)
        col_off = jnp.clip(w - qcol + (NA_COLS - 1), 0, NA_RPB_COLS - 1)
        col_start = jnp.clip(qcol - NA_COLS // 2, 0, GRID_W - NA_COLS)
        col_in = (w >= col_start) & (w < col_start + NA_COLS)
        left = lane < GRID_W

        def pair_body(k, carry):
            row = h * NA_RPB_ROWS + k
            acc = jnp.zeros((GRID_W, 2 * GRID_W), F32)
            for d in range(NA_RPB_COLS):
                val = jnp.where(left, rpb_ref[row, d], rpb_ref[row + 1, d])
                acc = jnp.where(col_off == d, val, acc)
            pairs_ref[k] = jnp.where(col_in, acc, MASK_VALUE)
            return carry

        lax.fori_loop(0, NA_PAIRS, pair_body, 0)

    kc = kc_ref[0]
    vc = vc_ref[0]

    lane = lax.broadcasted_iota(jnp.int32, (GRID_W, 2 * GRID_W), 1)
    left = lane < GRID_W
    nq = NA_QROWS * GRID_W
    nk = NA_UNION * GRID_W

    def scores(g):
        r0 = g * NA_QROWS
        u0 = jnp.clip(r0 - kh // 2, 0, rows_n - NA_UNION)
        q0 = pl.multiple_of(r0 * GRID_W, nq)
        k0 = pl.multiple_of(u0 * GRID_W, GRID_W)
        q = q_ref[0, pl.ds(q0, nq), :]
        kw = k_ref[0, pl.ds(k0, nk), :]
        vw = v_ref[0, pl.ds(k0, nk), :]
        bias_rows = []
        for rq in range(NA_QROWS):
            r = r0 + rq
            s0 = jnp.clip(r - kh // 2, 0, rows_n - kh)
            tiles = []
            for m in range(NA_UNION // 2):
                kr = u0 + 2 * m
                ok_l = (kr >= s0) & (kr < s0 + kh)
                ok_r = (kr + 1 >= s0) & (kr + 1 < s0 + kh)
                ok = jnp.where(left, ok_l.astype(jnp.int32), ok_r.astype(jnp.int32)) != 0
                pair = pairs_ref[jnp.clip(kr - r + (NA_ROWS - 1), 0, NA_PAIRS - 1)]
                tiles.append(jnp.where(ok, pair, MASK_VALUE))
            bias_rows.append(jnp.concatenate(tiles, axis=-1))
        bias = jnp.concatenate(bias_rows, axis=0)
        s_lat = lax.dot_general(q, kw, nt, preferred_element_type=F32) * scale + bias
        s_ctx = lax.dot_general(q, kc, nt, preferred_element_type=F32) * scale
        return s_lat, s_ctx, vw, q0

    def weights(s_lat, s_ctx):
        m = jnp.maximum(jnp.max(s_lat, axis=-1, keepdims=True), jnp.max(s_ctx, axis=-1, keepdims=True))
        e_lat = jnp.exp(s_lat - m)
        e_ctx = jnp.exp(s_ctx - m)
        den = jnp.sum(e_lat, axis=-1, keepdims=True) + jnp.sum(e_ctx, axis=-1, keepdims=True)
        return e_lat.astype(BF16), e_ctx.astype(BF16), den

    def body(t, carry):
        sc = [scores(t * NA_GROUP + u) for u in range(NA_GROUP)]
        ws = [weights(s[0], s[1]) for s in sc]
        for (s_lat, s_ctx, vw, q0), (e_lat, e_ctx, den) in zip(sc, ws):
            o = jnp.dot(e_lat, vw, preferred_element_type=F32) + jnp.dot(e_ctx, vc, preferred_element_type=F32)
            o_ref[0, pl.ds(q0, nq), :] = (o / den).astype(BF16)
        return carry

    lax.fori_loop(0, rows_n // (NA_QROWS * NA_GROUP), body, 0)


def _na_latent(p_l, p_c, rpb):
    b, l, _ = p_l.shape
    lc = p_c.shape[1]
    rows_n = l // GRID_W
    kh = NA_ROWS
    assert rpb.shape == (NA_HEADS, NA_RPB_ROWS, NA_RPB_COLS)
    assert rows_n >= NA_UNION and rows_n % (NA_QROWS * NA_GROUP) == 0 and (rows_n - NA_UNION) % 2 == 0

    def col(off, length):
        return pl.BlockSpec((1, length, NA_DH), lambda h, b: (b, 0, off // NA_DH + h))

    return pl.pallas_call(
        functools.partial(_na_kernel, rows_n=rows_n, kh=kh),
        grid=(NA_HEADS, b),
        in_specs=[pl.BlockSpec(memory_space=pltpu.SMEM),
                  col(OFF_NQ, l), col(OFF_NK, l), col(OFF_NV, l), col(OFF_NK, lc), col(OFF_NV, lc)],
        out_specs=pl.BlockSpec((1, l, NA_DH), lambda h, b: (b, 0, h)),
        out_shape=jax.ShapeDtypeStruct((b, l, NA_W), BF16),
        scratch_shapes=[pltpu.VMEM((NA_PAIRS, GRID_W, 2 * GRID_W), F32)],
        compiler_params=_params("parallel", "arbitrary"),
        name="na_latent",
    )(rpb.astype(F32).reshape(NA_HEADS * NA_RPB_ROWS, NA_RPB_COLS), p_l, p_l, p_l, p_c, p_c)


def _na_ctx_kernel(q_ref, k_ref, v_ref, o_ref):
    scale = NA_DH ** -0.5
    s = lax.dot_general(q_ref[0], k_ref[0], (((1,), (1,)), ((), ())), preferred_element_type=F32) * scale
    e = jnp.exp(s - jnp.max(s, axis=-1, keepdims=True))
    o = jnp.dot(e.astype(BF16), v_ref[0], preferred_element_type=F32)
    o_ref[0] = (o / jnp.sum(e, axis=-1, keepdims=True)).astype(BF16)


def _na_context(p_c):
    b, lc, _ = p_c.shape

    def col(off):
        return pl.BlockSpec((1, lc, NA_DH), lambda b, h: (b, 0, off // NA_DH + h))

    return pl.pallas_call(
        _na_ctx_kernel,
        grid=(b, NA_HEADS),
        in_specs=[col(OFF_NQ), col(OFF_NK), col(OFF_NV)],
        out_specs=pl.BlockSpec((1, lc, NA_DH), lambda b, h: (b, 0, h)),
        out_shape=jax.ShapeDtypeStruct((b, lc, NA_W), BF16),
        compiler_params=_params("parallel", "parallel"),
        name="na_context",
    )(p_c, p_c, p_c)


FFN_HALO = 16
FFN_SLICE = 256
def _ffn_kernel(x_ref, xp_ref, xn_ref, g_ref, sh_ref, sc_ref, gate_ref, upv_ref, upg_ref, wv_ref, wg_ref, bv_ref,
                bg_ref, down_ref, fg_ref, o_ref, h_ref, *, final_norm, seq):
    i = pl.program_id(1)
    j = pl.program_id(2)
    tm = x_ref.shape[1]

    def norm_mod(x):
        ms = jnp.mean(x * x, axis=-1, keepdims=True)
        gain = g_ref[...] * (1.0 + sc_ref[0])
        return x * lax.rsqrt(ms + EPS) * gain + sh_ref[0]

    @pl.when(j == 0)
    def _():
        h_ref[0:tm, :] = norm_mod(x_ref[0]).astype(BF16)
        hp = jnp.where(i == 0, 0.0, norm_mod(xp_ref[0]))
        hn = jnp.where(i == pl.num_programs(1) - 1, 0.0, norm_mod(xn_ref[0]))
        h_ref[tm:tm + FFN_HALO, :] = jnp.concatenate([hp, hn], axis=0).astype(BF16)
        o_ref[0] = jnp.zeros(o_ref.shape[1:], F32)

    h = h_ref[...]
    edge = lax.broadcasted_iota(jnp.int32, (SUBLANES, 1), 0)

    def conv3(u, w_ref, b_ref, c):
        x = u[0:tm]
        prev_row = u[tm + SUBLANES - 1:tm + SUBLANES]
        next_row = u[tm + SUBLANES:tm + SUBLANES + 1]
        x_prev = pltpu.roll(x, 1, 0)
        x_next = pltpu.roll(x, tm - 1, 0)
        if seq < tm:
            pos = lax.rem(lax.broadcasted_iota(jnp.int32, (tm, 1), 0), seq)
            x_prev = jnp.where(pos == 0, 0.0, x_prev)
            x_next = jnp.where(pos == seq - 1, 0.0, x_next)
        else:
            x_prev = jnp.concatenate([jnp.where(edge == 0, prev_row, x_prev[0:SUBLANES]), x_prev[SUBLANES:]],
                                     axis=0)
            x_next = jnp.concatenate([x_next[:tm - SUBLANES],
                                      jnp.where(edge == SUBLANES - 1, next_row, x_next[tm - SUBLANES:])], axis=0)
        return x_prev * w_ref[0:1, c] + x * w_ref[1:2, c] + x_next * w_ref[2:3, c] + b_ref[:, c]

    tf = upv_ref.shape[1]
    width = min(tf, FFN_SLICE)
    slices = [slice(s * width, (s + 1) * width) for s in range(tf // width)]
    us = []
    for c in slices:
        us.append((jnp.dot(h, upv_ref[:, c], preferred_element_type=F32),
                   jnp.dot(h, upg_ref[:, c], preferred_element_type=F32)))
    for (u_val, u_gate), c in zip(us, slices):
        val = conv3(u_val, wv_ref, bv_ref, c)
        gate = conv3(u_gate, wg_ref, bg_ref, c)
        act = (gate * _sigmoid(gate) * val).astype(BF16)
        o_ref[0] += jnp.dot(act, down_ref[c, :], preferred_element_type=F32)

    @pl.when(j == pl.num_programs(2) - 1)
    def _():
        y = x_ref[0] + gate_ref[0] * o_ref[0]
        if final_norm:
            ms = jnp.mean(y * y, axis=-1, keepdims=True)
            y = y * lax.rsqrt(ms + EPS) * fg_ref[...]
        o_ref[0] = y


def _ffn(x, g, mod, layer, row, up, dw_w, dw_b, down, final_g, final_norm, seq, name):
    b, l, d = x.shape
    f = down.shape[1]
    tm = min(l, 512)
    assert seq == l or (seq < tm and tm % seq == 0)
    tf = min(f, 512)
    nf = f // tf
    rb = tm // SUBLANES

    def mspec(chunk):
        return pl.BlockSpec((None, 1, 1, d), lambda b, i, j: (layer, row(b), 0, chunk))

    def wspec(rows, off):
        return pl.BlockSpec((rows, tf), lambda b, i, j: (0, off + j))

    in_specs = [
        pl.BlockSpec((1, tm, d), lambda b, i, j: (b, i, 0)),
        pl.BlockSpec((1, SUBLANES, d), lambda b, i, j: (b, jnp.maximum(i * rb - 1, 0), 0)),
        pl.BlockSpec((1, SUBLANES, d), lambda b, i, j: (b, jnp.minimum((i + 1) * rb, l // SUBLANES - 1), 0)),
        pl.BlockSpec((1, d), lambda b, i, j: (0, 0)),
        mspec(3), mspec(4), mspec(5),
        pl.BlockSpec((None, d, tf), lambda b, i, j: (layer, 0, j)),
        pl.BlockSpec((None, d, tf), lambda b, i, j: (layer, 0, nf + j)),
        wspec(FFN_K, 0), wspec(FFN_K, nf), wspec(1, 0), wspec(1, nf),
        pl.BlockSpec((None, tf, d), lambda b, i, j: (layer, j, 0)),
        pl.BlockSpec((1, d), lambda b, i, j: (0, 0)),
    ]
    return pl.pallas_call(
        functools.partial(_ffn_kernel, final_norm=final_norm, seq=seq),
        grid=(b, l // tm, nf),
        in_specs=in_specs,
        out_specs=pl.BlockSpec((1, tm, d), lambda b, i, j: (b, i, 0)),
        out_shape=jax.ShapeDtypeStruct((b, l, d), F32),
        scratch_shapes=[pltpu.VMEM((tm + FFN_HALO, d), BF16)],
        compiler_params=_params("parallel", "parallel", "arbitrary"),
        name=name,
    )(x, x, x, g.reshape(1, d), mod, mod, mod, up, up, dw_w, dw_w, dw_b.reshape(1, 2 * f), dw_b.reshape(1, 2 * f),
      down, final_g.reshape(1, d))


def _rope_tables(l):
    nf = RET_DK // 4
    pos = np.arange(l)
    inv = ROPE_BASE ** (-jnp.arange(nf, dtype=F32) / nf)
    ang_r = jnp.asarray(pos // GRID_W, F32)[:, None] * inv[None, :]
    ang_c = jnp.asarray(pos % GRID_W, F32)[:, None] * inv[None, :]
    cos = jnp.concatenate([jnp.cos(ang_r), jnp.cos(ang_r), jnp.cos(ang_c), jnp.cos(ang_c)], axis=-1)
    sin = jnp.concatenate([-jnp.sin(ang_r), jnp.sin(ang_r), -jnp.sin(ang_c), jnp.sin(ang_c)], axis=-1)
    return cos, sin


def kernel(x, c, ctx, c_ctx, w_ada, b_ada, norm1_g, w_in, ret_decay, ret_gn_g, conv_dw_w, conv_dw_b, conv_ln_g,
           conv_ln_b, conv_pw, na_rpb, w_out, norm2_g, ffn_up, ffn_dw_w, ffn_dw_b, ffn_down, final_g):
    depth = w_ada.shape[0]
    bsz, l, d = x.shape

    cond = jnp.zeros((COND_ROWS, d), F32).at[:bsz].set(c).at[bsz].set(c_ctx)
    mod = _adaln(cond, w_ada, b_ada).reshape(depth, COND_ROWS, 1, 6 * d)
    lat_row = lambda b: b
    ctx_row = lambda b: bsz
    cos, sin = _rope_tables(l)
    log_gamma = jax.nn.log_sigmoid(ret_decay.astype(F32))
    w_in_b = w_in.astype(BF16)
    w_out_b = w_out.astype(BF16)
    up_b = ffn_up.astype(BF16)
    down_b = ffn_down.astype(BF16)
    tn_in = w_in.shape[-1] // 4

    lc = ctx.shape[1]
    flat = lambda t: t.reshape(1, bsz * lc, t.shape[-1])
    h_ctx = flat(ctx)
    for layer in range(depth):
        last = layer == depth - 1
        pw_b = conv_pw[layer].astype(BF16)

        p_l = _norm_mod_matmul(x, norm1_g[layer], mod, layer, lat_row, 0, 1, w_in_b, tn_in, "in_proj")
        p_c = _norm_mod_matmul(h_ctx, norm1_g[layer], mod, layer, ctx_row, 0, 1, w_in_b, tn_in, "in_proj_ctx")
        p_c = p_c.reshape(bsz, lc, p_c.shape[-1])

        ret_l, ret_c = _retention(p_l, p_c, log_gamma[layer], ret_gn_g[layer], cos, sin, with_ctx=not last)
        conv_l = _conv_module(p_l, conv_dw_w[layer], conv_dw_b[layer], conv_ln_g[layer], conv_ln_b[layer], pw_b)
        na_l = _na_latent(p_l, p_c, na_rpb[layer])
        x = _matmul_residual([ret_l, conv_l, na_l], w_out_b, x, mod, layer, lat_row, 2, "out_proj")

        x = _ffn(x, norm2_g[layer], mod, layer, lat_row, up_b, ffn_dw_w[layer], ffn_dw_b[layer], down_b, final_g,
                 last, l, "ffn")

        if not last:
            conv_c = _conv_module(p_c, conv_dw_w[layer], conv_dw_b[layer], conv_ln_g[layer], conv_ln_b[layer], pw_b)
            na_c = _na_context(p_c)
            h_ctx = _matmul_residual([flat(ret_c), flat(conv_c), flat(na_c)], w_out_b, h_ctx, mod, layer, ctx_row, 2,
                                     "out_proj_ctx")
            h_ctx = _ffn(h_ctx, norm2_g[layer], mod, layer, ctx_row, up_b, ffn_dw_w[layer], ffn_dw_b[layer], down_b,
                         final_g, False, lc, "ffn_ctx")

    return x
```

```python
import functools

import numpy as np
import jax
import jax.numpy as jnp
from jax import lax
from jax.experimental import pallas as pl
from jax.experimental.pallas import tpu as pltpu

F32 = jnp.float32
BF16 = jnp.bfloat16

GRID_W = 64
RET_HEADS = 4
RET_DK = 128
RET_DV = 256
RET_CHUNK = 128
RET_UNROLL = 8
CONV_W = 512
CONV_K = 31
NA_HEADS = 4
NA_DH = 128
NA_ROWS = 8
NA_COLS = 16
FFN_K = 3
ROPE_BASE = 10000.0
EPS = 1e-6

RET_QK_W = RET_HEADS * RET_DK
RET_W = RET_HEADS * RET_DV
NA_W = NA_HEADS * NA_DH
OFF_RQ = 0
OFF_RK = OFF_RQ + RET_QK_W
OFF_RV = OFF_RK + RET_QK_W
OFF_RG = OFF_RV + RET_W
OFF_CA = OFF_RG + RET_W
OFF_CB = OFF_CA + CONV_W
OFF_NQ = OFF_CB + CONV_W
OFF_NK = OFF_NQ + NA_W
OFF_NV = OFF_NK + NA_W

V7X_VMEM_BYTES = 64 * 1024 * 1024
VMEM_LIMIT = V7X_VMEM_BYTES - 8 * 1024 * 1024
SUBLANES = 8
LANES = 128
COND_ROWS = 8
MASK_VALUE = -1e30
CONV_PAD = 16
CONV_TILE = 64


def _sigmoid(x):
    return 1.0 / (1.0 + jnp.exp(-x))


def _params(*sem):
    return pltpu.CompilerParams(dimension_semantics=sem, vmem_limit_bytes=VMEM_LIMIT)


def _adaln_kernel(c_ref, w_ref, b_ref, o_ref):
    c = c_ref[...]
    s = (c * _sigmoid(c)).astype(BF16)
    o_ref[0] = jnp.dot(s, w_ref[0].astype(BF16), preferred_element_type=F32) + b_ref[0]


def _adaln(cond, w_ada, b_ada):
    depth, d, n = w_ada.shape
    tn = 1024
    return pl.pallas_call(
        _adaln_kernel,
        grid=(depth, n // tn),
        in_specs=[
            pl.BlockSpec((COND_ROWS, d), lambda l, j: (0, 0)),
            pl.BlockSpec((1, d, tn), lambda l, j: (l, 0, j)),
            pl.BlockSpec((1, 1, tn), lambda l, j: (l, 0, j)),
        ],
        out_specs=pl.BlockSpec((1, COND_ROWS, tn), lambda l, j: (l, 0, j)),
        out_shape=jax.ShapeDtypeStruct((depth, COND_ROWS, n), F32),
        compiler_params=_params("parallel", "parallel"),
        name="adaln",
    )(cond, w_ada, b_ada.reshape(depth, 1, n))


def _mod_spec(layer, d, chunk, row):
    return pl.BlockSpec((None, 1, 1, d), lambda b, i, j: (layer, row(b), 0, chunk))


def _norm_mod_matmul_kernel(x_ref, g_ref, sh_ref, sc_ref, w_ref, o_ref, h_ref):
    @pl.when(pl.program_id(2) == 0)
    def _():
        x = x_ref[0]
        ms = jnp.mean(x * x, axis=-1, keepdims=True)
        gain = g_ref[...] * (1.0 + sc_ref[0])
        h_ref[...] = (x * lax.rsqrt(ms + EPS) * gain + sh_ref[0]).astype(BF16)

    o_ref[0] = jnp.dot(h_ref[...], w_ref[...], preferred_element_type=F32).astype(o_ref.dtype)


def _norm_mod_matmul(x, g, mod, layer, row, shift_chunk, scale_chunk, w, tn, name):
    b, l, d = x.shape
    n = w.shape[-1]
    tm = min(l, 1024)
    return pl.pallas_call(
        _norm_mod_matmul_kernel,
        grid=(b, l // tm, n // tn),
        in_specs=[
            pl.BlockSpec((1, tm, d), lambda b, i, j: (b, i, 0)),
            pl.BlockSpec((1, d), lambda b, i, j: (0, 0)),
            _mod_spec(layer, d, shift_chunk, row),
            _mod_spec(layer, d, scale_chunk, row),
            pl.BlockSpec((None, d, tn), lambda b, i, j: (layer, 0, j)),
        ],
        out_specs=pl.BlockSpec((1, tm, tn), lambda b, i, j: (b, i, j)),
        out_shape=jax.ShapeDtypeStruct((b, l, n), BF16),
        scratch_shapes=[pltpu.VMEM((tm, d), BF16)],
        compiler_params=_params("parallel", "parallel", "arbitrary"),
        name=name,
    )(x, g.reshape(1, d), mod, mod, w)


def _matmul_residual_kernel(*refs, n_a):
    a_refs = refs[:n_a]
    w_ref, res_ref, gate_ref, o_ref = refs[n_a:]
    acc = None
    off = 0
    for a_ref in a_refs:
        k = a_ref.shape[-1]
        part = jnp.dot(a_ref[0], w_ref[off:off + k, :], preferred_element_type=F32)
        acc = part if acc is None else acc + part
        off += k
    o_ref[0] = res_ref[0] + gate_ref[0] * acc


def _matmul_residual(a_list, w, res, mod, layer, row, gate_chunk, name):
    b, l, n = res.shape
    k_total = w.shape[1]
    tm = min(l, 512)
    tn = n
    in_specs = [pl.BlockSpec((1, tm, a.shape[-1]), lambda b, i, j: (b, i, 0)) for a in a_list]
    in_specs += [
        pl.BlockSpec((None, k_total, tn), lambda b, i, j: (layer, 0, j)),
        pl.BlockSpec((1, tm, tn), lambda b, i, j: (b, i, j)),
        pl.BlockSpec((None, 1, 1, tn), lambda b, i, j: (layer, row(b), 0, gate_chunk * (n // tn) + j)),
    ]
    return pl.pallas_call(
        functools.partial(_matmul_residual_kernel, n_a=len(a_list)),
        grid=(b, l // tm, n // tn),
        in_specs=in_specs,
        out_specs=pl.BlockSpec((1, tm, tn), lambda b, i, j: (b, i, j)),
        out_shape=jax.ShapeDtypeStruct((b, l, n), F32),
        compiler_params=_params("parallel", "parallel", "arbitrary"),
        name=name,
    )(*a_list, w, res, mod)


def _rope(t, cos, sin):
    lane = lax.broadcasted_iota(jnp.int32, t.shape, 1)
    partner = jnp.where((lane & 32) == 0, pltpu.roll(t, LANES - 32, 1), pltpu.roll(t, 32, 1))
    return t * cos + partner * sin


def _retention_kernel(lg_ref, ql_ref, kl_ref, vl_ref, gl_ref, qc_ref, kc_ref, vc_ref, gc_ref, cos_ref, sin_ref,
                      gn_ref, *rest, with_ctx, n_lat, n_ctx):
    if with_ctx:
        ol_ref, oc_ref, acc_ref, qr_ref, kr_ref, accc_ref = rest
    else:
        ol_ref, acc_ref, qr_ref, kr_ref = rest
    c = RET_CHUNK
    scale = RET_DK ** -0.5
    h = pl.program_id(1)
    lgf = lg_ref[0, h]
    lgb = lg_ref[1, h]
    ii = lax.broadcasted_iota(jnp.int32, (c, c), 0).astype(F32)
    jj = lax.broadcasted_iota(jnp.int32, (c, c), 1).astype(F32)
    diff = ii - jj
    decay = (jnp.where(diff >= 0, jnp.exp(jnp.maximum(diff, 0.0) * lgf), 0.0)
             + jnp.where(diff <= 0, jnp.exp(jnp.maximum(-diff, 0.0) * lgb), 0.0))
    ic = lax.broadcasted_iota(jnp.int32, (c, 1), 0).astype(F32)
    xi_f = jnp.exp((ic + 1.0) * lgf)
    zeta_f = jnp.exp((c - 1.0 - ic) * lgf)
    xi_b = jnp.exp((c - ic) * lgb)
    zeta_b = jnp.exp(ic * lgb)
    g_f = jnp.exp(jnp.full((1, RET_DV), float(c), F32) * lgf)
    g_b = jnp.exp(jnp.full((1, RET_DV), float(c), F32) * lgb)
    gn = gn_ref[...]

    def intra_and_inter(qb, kb, v, xi, st):
        s = lax.dot_general(qb, kb, (((1,), (1,)), ((), ())), preferred_element_type=F32)
        p = (s * decay).astype(BF16)
        inter = jnp.dot(qb, st.astype(BF16), preferred_element_type=F32) * xi
        return jnp.dot(p, v, preferred_element_type=F32) + inter

    def next_state(st, k_f32, v, zeta, g):
        kz = (k_f32 * zeta).T.astype(BF16)
        return st * g + jnp.dot(kz, v, preferred_element_type=F32)

    def finish(total, gate):
        mu = jnp.mean(total, axis=-1, keepdims=True)
        d = total - mu
        var = jnp.mean(d * d, axis=-1, keepdims=True)
        y = d * lax.rsqrt(var + EPS) * gn
        return (y * (gate * _sigmoid(gate))).astype(BF16)

    st = jnp.zeros((RET_DK, RET_DV), F32)
    for j in range(n_ctx):
        rows = slice(j * c, (j + 1) * c)
        q = qc_ref[0, rows, :].astype(F32) * scale
        k = kc_ref[0, rows, :].astype(F32)
        v = vc_ref[0, rows, :]
        if with_ctx:
            accc_ref[rows, :] = intra_and_inter(q.astype(BF16), k.astype(BF16), v, xi_f, st)
        st = next_state(st, k, v, zeta_f, g_f)

    nt = (((1,), (1,)), ((), ()))
    group = RET_UNROLL
    assert n_lat % group == 0

    def fwd_body(jg, st):
        rows, qb, kb, v, kzv = [], [], [], [], []
        for u in range(group):
            r = pl.ds(pl.multiple_of((jg * group + u) * c, c), c)
            cos = cos_ref[r, :]
            sin = sin_ref[r, :]
            q = _rope(ql_ref[0, r, :].astype(F32), cos, sin) * scale
            k = _rope(kl_ref[0, r, :].astype(F32), cos, sin)
            rows.append(r)
            qb.append(q.astype(BF16))
            kb.append(k.astype(BF16))
            v.append(vl_ref[0, r, :])
            qr_ref[r, :] = qb[u]
            kr_ref[r, :] = kb[u]
            kzv.append(jnp.dot((k * zeta_f).T.astype(BF16), v[u], preferred_element_type=F32))
        s = [lax.dot_general(qb[u], kb[u], nt, preferred_element_type=F32) for u in range(group)]
        states = []
        for u in range(group):
            states.append(st.astype(BF16))
            st = st * g_f + kzv[u]
        p = [(s[u] * decay).astype(BF16) for u in range(group)]
        for u in range(group):
            inter = jnp.dot(qb[u], states[u], preferred_element_type=F32) * xi_f
            acc_ref[rows[u], :] = jnp.dot(p[u], v[u], preferred_element_type=F32) + inter
        return st

    lax.fori_loop(0, n_lat // group, fwd_body, st)

    st = jnp.zeros((RET_DK, RET_DV), F32)
    for j in reversed(range(n_ctx)):
        rows = slice(j * c, (j + 1) * c)
        k = kc_ref[0, rows, :].astype(F32)
        v = vc_ref[0, rows, :]
        if with_ctx:
            qb = (qc_ref[0, rows, :].astype(F32) * scale).astype(BF16)
            inter = jnp.dot(qb, st.astype(BF16), preferred_element_type=F32) * xi_b
            oc_ref[0, rows, :] = finish(accc_ref[rows, :] + inter, gc_ref[0, rows, :].astype(F32))
        st = next_state(st, k, v, zeta_b, g_b)

    def bwd_body(jg, st):
        rows, kzv = [], []
        for u in range(group):
            r = pl.ds(pl.multiple_of((n_lat - 1 - (jg * group + u)) * c, c), c)
            rows.append(r)
            kz = (kr_ref[r, :].astype(F32) * zeta_b).T.astype(BF16)
            kzv.append(jnp.dot(kz, vl_ref[0, r, :], preferred_element_type=F32))
        states = []
        for u in range(group):
            states.append(st.astype(BF16))
            st = st * g_b + kzv[u]
        inter = [jnp.dot(qr_ref[rows[u], :], states[u], preferred_element_type=F32) * xi_b for u in range(group)]
        for u in range(group):
            ol_ref[0, rows[u], :] = finish(acc_ref[rows[u], :] + inter[u], gl_ref[0, rows[u], :].astype(F32))
        return st

    lax.fori_loop(0, n_lat // group, bwd_body, st)


def _retention(p_l, p_c, log_gamma, gn_g, cos, sin, with_ctx):
    b, l, _ = p_l.shape
    lc = p_c.shape[1]
    c = RET_CHUNK

    def col(width, off, length):
        return pl.BlockSpec((1, length, width), lambda b, h: (b, 0, off // width + h))

    in_specs = [
        pl.BlockSpec(memory_space=pltpu.SMEM),
        col(RET_DK, OFF_RQ, l), col(RET_DK, OFF_RK, l), col(RET_DV, OFF_RV, l), col(RET_DV, OFF_RG, l),
        col(RET_DK, OFF_RQ, lc), col(RET_DK, OFF_RK, lc), col(RET_DV, OFF_RV, lc), col(RET_DV, OFF_RG, lc),
        pl.BlockSpec((l, RET_DK), lambda b, h: (0, 0)),
        pl.BlockSpec((l, RET_DK), lambda b, h: (0, 0)),
        pl.BlockSpec((1, RET_DV), lambda b, h: (0, h)),
    ]
    out_specs = [pl.BlockSpec((1, l, RET_DV), lambda b, h: (b, 0, h))]
    out_shape = [jax.ShapeDtypeStruct((b, l, RET_W), BF16)]
    scratch = [pltpu.VMEM((l, RET_DV), F32), pltpu.VMEM((l, RET_DK), BF16), pltpu.VMEM((l, RET_DK), BF16)]
    if with_ctx:
        out_specs.append(pl.BlockSpec((1, lc, RET_DV), lambda b, h: (b, 0, h)))
        out_shape.append(jax.ShapeDtypeStruct((b, lc, RET_W), BF16))
        scratch.append(pltpu.VMEM((lc, RET_DV), F32))
    outs = pl.pallas_call(
        functools.partial(_retention_kernel, with_ctx=with_ctx, n_lat=l // c, n_ctx=lc // c),
        grid=(b, RET_HEADS),
        in_specs=in_specs,
        out_specs=out_specs,
        out_shape=out_shape,
        scratch_shapes=scratch,
        compiler_params=_params("parallel", "parallel"),
        name="retention_ctx" if with_ctx else "retention",
    )(log_gamma, p_l, p_l, p_l, p_l, p_c, p_c, p_c, p_c, cos, sin, gn_g.reshape(1, RET_W))
    return (outs[0], outs[1]) if with_ctx else (outs[0], None)


def _conv_kernel(a_ref, b_ref, dww_ref, dwb_ref, lng_ref, lnb_ref, pw_ref, o_ref, upad_ref, act_ref, *, l):
    pad = CONV_PAD
    upad_ref[0:pad, :] = jnp.zeros((pad, CONV_W), F32)
    upad_ref[pad + l:pad + l + pad, :] = jnp.zeros((pad, CONV_W), F32)
    tg = min(l, 256)

    def glu_body(t, carry):
        r0 = pl.multiple_of(t * tg, tg)
        a = a_ref[0, pl.ds(r0, tg), :].astype(F32)
        b = b_ref[0, pl.ds(r0, tg), :].astype(F32)
        upad_ref[pl.ds(pad + r0, tg), :] = a * _sigmoid(b)
        return carry

    lax.fori_loop(0, l // tg, glu_body, 0)

    tile = CONV_TILE
    halo = 2 * pad
    lng = lng_ref[...]
    lnb = lnb_ref[...]

    def conv_body(t, carry):
        r0 = pl.multiple_of(t * tile, tile)
        ys = []
        for cb in range(CONV_W // LANES):
            cols = slice(cb * LANES, (cb + 1) * LANES)
            win = upad_ref[pl.ds(r0, tile + halo), cols]
            acc = jnp.zeros((tile, LANES), F32) + dwb_ref[:, cols]
            for r in range(SUBLANES):
                rolled = win if r == 0 else pltpu.roll(win, tile + halo - r, 0)
                for a in range(halo // SUBLANES):
                    k = SUBLANES * a + r - 1
                    if 0 <= k < CONV_K:
                        acc = acc + rolled[SUBLANES * a:SUBLANES * a + tile] * dww_ref[k:k + 1, cols]
            ys.append(acc)
        y = jnp.concatenate(ys, axis=-1)
        mu = jnp.mean(y, axis=-1, keepdims=True)
        d = y - mu
        var = jnp.mean(d * d, axis=-1, keepdims=True)
        u = d * lax.rsqrt(var + EPS) * lng + lnb
        act_ref[pl.ds(r0, tile), :] = (u * _sigmoid(u)).astype(BF16)
        return carry

    lax.fori_loop(0, l // tile, conv_body, 0)

    tmm = min(l, 512)

    def mm_body(t, carry):
        r0 = pl.multiple_of(t * tmm, tmm)
        o_ref[0, pl.ds(r0, tmm), :] = jnp.dot(act_ref[pl.ds(r0, tmm), :], pw_ref[...],
                                              preferred_element_type=F32).astype(BF16)
        return carry

    lax.fori_loop(0, l // tmm, mm_body, 0)


def _conv_module(p, dw_w, dw_b, ln_g, ln_b, pw):
    b, l, _ = p.shape
    vec = pl.BlockSpec((1, CONV_W), lambda b: (0, 0))
    return pl.pallas_call(
        functools.partial(_conv_kernel, l=l),
        grid=(b,),
        in_specs=[
            pl.BlockSpec((1, l, CONV_W), lambda b: (b, 0, OFF_CA // CONV_W)),
            pl.BlockSpec((1, l, CONV_W), lambda b: (b, 0, OFF_CB // CONV_W)),
            pl.BlockSpec((CONV_K, CONV_W), lambda b: (0, 0)),
            vec, vec, vec,
            pl.BlockSpec((CONV_W, CONV_W), lambda b: (0, 0)),
        ],
        out_specs=pl.BlockSpec((1, l, CONV_W), lambda b: (b, 0, 0)),
        out_shape=jax.ShapeDtypeStruct((b, l, CONV_W), BF16),
        scratch_shapes=[pltpu.VMEM((l + 2 * CONV_PAD, CONV_W), F32),
                        pltpu.VMEM((l, CONV_W), BF16)],
        compiler_params=_params("parallel"),
        name="conv_module",
    )(p, p, dw_w, dw_b.reshape(1, CONV_W), ln_g.reshape(1, CONV_W), ln_b.reshape(1, CONV_W), pw)


NA_RPB_ROWS = 2 * NA_ROWS - 1
NA_RPB_COLS = 2 * NA_COLS - 1
NA_PAIRS = NA_RPB_ROWS - 1
NA_QROWS = 4
NA_UNION = NA_ROWS + NA_QROWS
NA_GROUP = 2


def _na_kernel(rpb_ref, q_ref, k_ref, v_ref, kc_ref, vc_ref, o_ref, pairs_ref, *, rows_n, kh):
    scale = NA_DH ** -0.5
    h = pl.program_id(0)
    nt = (((1,), (1,)), ((), ()))

    @pl.when(pl.program_id(1) == 0)
    def _build_bias():
        lane = lax.broadcasted_iota(jnp.int32, (GRID_W, 2 * GRID_W), 1)
        qcol = lax.broadcasted_iota(jnp.int32, (GRID_W, 2 * GRID_W), 0)
        w = lane & (GRID_W - 1)
        col_off = jnp.clip(w - qcol + (NA_COLS - 1), 0, NA_RPB_COLS - 1)
        col_start = jnp.clip(qcol - NA_COLS // 2, 0, GRID_W - NA_COLS)
        col_in = (w >= col_start) & (w < col_start + NA_COLS)
        left = lane < GRID_W

        def pair_body(k, carry):
            row = h * NA_RPB_ROWS + k
            acc = jnp.zeros((GRID_W, 2 * GRID_W), F32)
            for d in range(NA_RPB_COLS):
                val = jnp.where(left, rpb_ref[row, d], rpb_ref[row + 1, d])
                acc = jnp.where(col_off == d, val, acc)
            pairs_ref[k] = jnp.where(col_in, acc, MASK_VALUE)
            return carry

        lax.fori_loop(0, NA_PAIRS, pair_body, 0)

    kc = kc_ref[0]
    vc = vc_ref[0]

    lane = lax.broadcasted_iota(jnp.int32, (GRID_W, 2 * GRID_W), 1)
    left = lane < GRID_W
    nq = NA_QROWS * GRID_W
    nk = NA_UNION * GRID_W

    def scores(g):
        r0 = g * NA_QROWS
        u0 = jnp.clip(r0 - kh // 2, 0, rows_n - NA_UNION)
        q0 = pl.multiple_of(r0 * GRID_W, nq)
        k0 = pl.multiple_of(u0 * GRID_W, GRID_W)
        q = q_ref[0, pl.ds(q0, nq), :]
        kw = k_ref[0, pl.ds(k0, nk), :]
        vw = v_ref[0, pl.ds(k0, nk), :]
        bias_rows = []
        for rq in range(NA_QROWS):
            r = r0 + rq
            s0 = jnp.clip(r - kh // 2, 0, rows_n - kh)
            tiles = []
            for m in range(NA_UNION // 2):
                kr = u0 + 2 * m
                ok_l = (kr >= s0) & (kr < s0 + kh)
                ok_r = (kr + 1 >= s0) & (kr + 1 < s0 + kh)
                ok = jnp.where(left, ok_l.astype(jnp.int32), ok_r.astype(jnp.int32)) != 0
                pair = pairs_ref[jnp.clip(kr - r + (NA_ROWS - 1), 0, NA_PAIRS - 1)]
                tiles.append(jnp.where(ok, pair, MASK_VALUE))
            bias_rows.append(jnp.concatenate(tiles, axis=-1))
        bias = jnp.concatenate(bias_rows, axis=0)
        s_lat = lax.dot_general(q, kw, nt, preferred_element_type=F32) * scale + bias
        s_ctx = lax.dot_general(q, kc, nt, preferred_element_type=F32) * scale
        return s_lat, s_ctx, vw, q0

    def weights(s_lat, s_ctx):
        m = jnp.maximum(jnp.max(s_lat, axis=-1, keepdims=True), jnp.max(s_ctx, axis=-1, keepdims=True))
        e_lat = jnp.exp(s_lat - m)
        e_ctx = jnp.exp(s_ctx - m)
        den = jnp.sum(e_lat, axis=-1, keepdims=True) + jnp.sum(e_ctx, axis=-1, keepdims=True)
        return e_lat.astype(BF16), e_ctx.astype(BF16), den

    def body(t, carry):
        sc = [scores(t * NA_GROUP + u) for u in range(NA_GROUP)]
        ws = [weights(s[0], s[1]) for s in sc]
        for (s_lat, s_ctx, vw, q0), (e_lat, e_ctx, den) in zip(sc, ws):
            o = jnp.dot(e_lat, vw, preferred_element_type=F32) + jnp.dot(e_ctx, vc, preferred_element_type=F32)
            o_ref[0, pl.ds(q0, nq), :] = (o / den).astype(BF16)
        return carry

    lax.fori_loop(0, rows_n // (NA_QROWS * NA_GROUP), body, 0)


def _na_latent(p_l, p_c, rpb):
    b, l, _ = p_l.shape
    lc = p_c.shape[1]
    rows_n = l // GRID_W
    kh = NA_ROWS
    assert rpb.shape == (NA_HEADS, NA_RPB_ROWS, NA_RPB_COLS)
    assert rows_n >= NA_UNION and rows_n % (NA_QROWS * NA_GROUP) == 0 and (rows_n - NA_UNION) % 2 == 0

    def col(off, length):
        return pl.BlockSpec((1, length, NA_DH), lambda h, b: (b, 0, off // NA_DH + h))

    return pl.pallas_call(
        functools.partial(_na_kernel, rows_n=rows_n, kh=kh),
        grid=(NA_HEADS, b),
        in_specs=[pl.BlockSpec(memory_space=pltpu.SMEM),
                  col(OFF_NQ, l), col(OFF_NK, l), col(OFF_NV, l), col(OFF_NK, lc), col(OFF_NV, lc)],
        out_specs=pl.BlockSpec((1, l, NA_DH), lambda h, b: (b, 0, h)),
        out_shape=jax.ShapeDtypeStruct((b, l, NA_W), BF16),
        scratch_shapes=[pltpu.VMEM((NA_PAIRS, GRID_W, 2 * GRID_W), F32)],
        compiler_params=_params("parallel", "arbitrary"),
        name="na_latent",
    )(rpb.astype(F32).reshape(NA_HEADS * NA_RPB_ROWS, NA_RPB_COLS), p_l, p_l, p_l, p_c, p_c)


def _na_ctx_kernel(q_ref, k_ref, v_ref, o_ref):
    scale = NA_DH ** -0.5
    s = lax.dot_general(q_ref[0], k_ref[0], (((1,), (1,)), ((), ())), preferred_element_type=F32) * scale
    e = jnp.exp(s - jnp.max(s, axis=-1, keepdims=True))
    o = jnp.dot(e.astype(BF16), v_ref[0], preferred_element_type=F32)
    o_ref[0] = (o / jnp.sum(e, axis=-1, keepdims=True)).astype(BF16)


def _na_context(p_c):
    b, lc, _ = p_c.shape

    def col(off):
        return pl.BlockSpec((1, lc, NA_DH), lambda b, h: (b, 0, off // NA_DH + h))

    return pl.pallas_call(
        _na_ctx_kernel,
        grid=(b, NA_HEADS),
        in_specs=[col(OFF_NQ), col(OFF_NK), col(OFF_NV)],
        out_specs=pl.BlockSpec((1, lc, NA_DH), lambda b, h: (b, 0, h)),
        out_shape=jax.ShapeDtypeStruct((b, lc, NA_W), BF16),
        compiler_params=_params("parallel", "parallel"),
        name="na_context",
    )(p_c, p_c, p_c)


FFN_HALO = 16
FFN_SLICE = 256
def _ffn_kernel(x_ref, xp_ref, xn_ref, g_ref, sh_ref, sc_ref, gate_ref, upv_ref, upg_ref, cw_ref, down_ref, fg_ref,
                o_ref, h_ref, *, final_norm, seq):
    i = pl.program_id(1)
    j = pl.program_id(2)
    tm = x_ref.shape[1]

    def norm_mod(x):
        ms = jnp.mean(x * x, axis=-1, keepdims=True)
        gain = g_ref[...] * (1.0 + sc_ref[0])
        return x * lax.rsqrt(ms + EPS) * gain + sh_ref[0]

    @pl.when(j == 0)
    def _():
        h_ref[0:tm, :] = norm_mod(x_ref[0]).astype(BF16)
        hp = jnp.where(i == 0, 0.0, norm_mod(xp_ref[0]))
        hn = jnp.where(i == pl.num_programs(1) - 1, 0.0, norm_mod(xn_ref[0]))
        h_ref[tm:tm + FFN_HALO, :] = jnp.concatenate([hp, hn], axis=0).astype(BF16)
        o_ref[0] = jnp.zeros(o_ref.shape[1:], F32)

    h = h_ref[...]
    edge = lax.broadcasted_iota(jnp.int32, (SUBLANES, 1), 0)

    def conv3(u, t0, b0, c):
        x = u[0:tm]
        prev_row = u[tm + SUBLANES - 1:tm + SUBLANES]
        next_row = u[tm + SUBLANES:tm + SUBLANES + 1]
        x_prev = pltpu.roll(x, 1, 0)
        x_next = pltpu.roll(x, tm - 1, 0)
        if seq < tm:
            pos = lax.rem(lax.broadcasted_iota(jnp.int32, (tm, 1), 0), seq)
            x_prev = jnp.where(pos == 0, 0.0, x_prev)
            x_next = jnp.where(pos == seq - 1, 0.0, x_next)
        else:
            x_prev = jnp.concatenate([jnp.where(edge == 0, prev_row, x_prev[0:SUBLANES]), x_prev[SUBLANES:]],
                                     axis=0)
            x_next = jnp.concatenate([x_next[:tm - SUBLANES],
                                      jnp.where(edge == SUBLANES - 1, next_row, x_next[tm - SUBLANES:])], axis=0)
        return (x_prev * cw_ref[t0:t0 + 1, c] + x * cw_ref[t0 + 1:t0 + 2, c] + x_next * cw_ref[t0 + 2:t0 + 3, c]
                + cw_ref[b0:b0 + 1, c])

    tf = upv_ref.shape[1]
    width = min(tf, FFN_SLICE)
    slices = [slice(s * width, (s + 1) * width) for s in range(tf // width)]
    us = []
    for c in slices:
        us.append((jnp.dot(h, upv_ref[:, c], preferred_element_type=F32),
                   jnp.dot(h, upg_ref[:, c], preferred_element_type=F32)))
    for (u_val, u_gate), c in zip(us, slices):
        val = conv3(u_val, 0, 2 * FFN_K, c)
        gate = conv3(u_gate, FFN_K, 2 * FFN_K + 1, c)
        act = (gate * _sigmoid(gate) * val).astype(BF16)
        o_ref[0] += jnp.dot(act, down_ref[c, :], preferred_element_type=F32)

    @pl.when(j == pl.num_programs(2) - 1)
    def _():
        y = x_ref[0] + gate_ref[0] * o_ref[0]
        if final_norm:
            ms = jnp.mean(y * y, axis=-1, keepdims=True)
            y = y * lax.rsqrt(ms + EPS) * fg_ref[...]
        o_ref[0] = y


def _ffn(x, g, mod, layer, row, up, dw_w, dw_b, down, final_g, final_norm, seq, name):
    b, l, d = x.shape
    f = down.shape[1]
    tm = min(l, 512)
    assert seq == l or (seq < tm and tm % seq == 0)
    tf = min(f, 512)
    nf = f // tf
    rb = tm // SUBLANES

    def mspec(chunk):
        return pl.BlockSpec((None, 1, 1, d), lambda b, i, j: (layer, row(b), 0, chunk))

    cw = jnp.concatenate([dw_w[:, :f], dw_w[:, f:], dw_b[None, :f], dw_b[None, f:]], axis=0)

    in_specs = [
        pl.BlockSpec((1, tm, d), lambda b, i, j: (b, i, 0)),
        pl.BlockSpec((1, SUBLANES, d), lambda b, i, j: (b, jnp.maximum(i * rb - 1, 0), 0)),
        pl.BlockSpec((1, SUBLANES, d), lambda b, i, j: (b, jnp.minimum((i + 1) * rb, l // SUBLANES - 1), 0)),
        pl.BlockSpec((1, d), lambda b, i, j: (0, 0)),
        mspec(3), mspec(4), mspec(5),
        pl.BlockSpec((None, d, tf), lambda b, i, j: (layer, 0, j)),
        pl.BlockSpec((None, d, tf), lambda b, i, j: (layer, 0, nf + j)),
        pl.BlockSpec((2 * FFN_K + 2, tf), lambda b, i, j: (0, j)),
        pl.BlockSpec((None, tf, d), lambda b, i, j: (layer, j, 0)),
        pl.BlockSpec((1, d), lambda b, i, j: (0, 0)),
    ]
    return pl.pallas_call(
        functools.partial(_ffn_kernel, final_norm=final_norm, seq=seq),
        grid=(b, l // tm, nf),
        in_specs=in_specs,
        out_specs=pl.BlockSpec((1, tm, d), lambda b, i, j: (b, i, 0)),
        out_shape=jax.ShapeDtypeStruct((b, l, d), F32),
        scratch_shapes=[pltpu.VMEM((tm + FFN_HALO, d), BF16)],
        compiler_params=_params("parallel", "parallel", "arbitrary"),
        name=name,
    )(x, x, x, g.reshape(1, d), mod, mod, mod, up, up, cw, down, final_g.reshape(1, d))


def _rope_tables(l):
    nf = RET_DK // 4
    pos = np.arange(l)
    inv = ROPE_BASE ** (-jnp.arange(nf, dtype=F32) / nf)
    ang_r = jnp.asarray(pos // GRID_W, F32)[:, None] * inv[None, :]
    ang_c = jnp.asarray(pos % GRID_W, F32)[:, None] * inv[None, :]
    cos = jnp.concatenate([jnp.cos(ang_r), jnp.cos(ang_r), jnp.cos(ang_c), jnp.cos(ang_c)], axis=-1)
    sin = jnp.concatenate([-jnp.sin(ang_r), jnp.sin(ang_r), -jnp.sin(ang_c), jnp.sin(ang_c)], axis=-1)
    return cos, sin


def kernel(x, c, ctx, c_ctx, w_ada, b_ada, norm1_g, w_in, ret_decay, ret_gn_g, conv_dw_w, conv_dw_b, conv_ln_g,
           conv_ln_b, conv_pw, na_rpb, w_out, norm2_g, ffn_up, ffn_dw_w, ffn_dw_b, ffn_down, final_g):
    depth = w_ada.shape[0]
    bsz, l, d = x.shape

    cond = jnp.zeros((COND_ROWS, d), F32).at[:bsz].set(c).at[bsz].set(c_ctx)
    mod = _adaln(cond, w_ada, b_ada).reshape(depth, COND_ROWS, 1, 6 * d)
    lat_row = lambda b: b
    ctx_row = lambda b: bsz
    cos, sin = _rope_tables(l)
    log_gamma = jax.nn.log_sigmoid(ret_decay.astype(F32))
    w_in_b = w_in.astype(BF16)
    w_out_b = w_out.astype(BF16)
    up_b = ffn_up.astype(BF16)
    down_b = ffn_down.astype(BF16)
    tn_in = w_in.shape[-1] // 4

    lc = ctx.shape[1]
    flat = lambda t: t.reshape(1, bsz * lc, t.shape[-1])
    h_ctx = flat(ctx)
    for layer in range(depth):
        last = layer == depth - 1
        pw_b = conv_pw[layer].astype(BF16)

        p_l = _norm_mod_matmul(x, norm1_g[layer], mod, layer, lat_row, 0, 1, w_in_b, tn_in, "in_proj")
        p_c = _norm_mod_matmul(h_ctx, norm1_g[layer], mod, layer, ctx_row, 0, 1, w_in_b, tn_in, "in_proj_ctx")
        p_c = p_c.reshape(bsz, lc, p_c.shape[-1])

        ret_l, ret_c = _retention(p_l, p_c, log_gamma[layer], ret_gn_g[layer], cos, sin, with_ctx=not last)
        conv_l = _conv_module(p_l, conv_dw_w[layer], conv_dw_b[layer], conv_ln_g[layer], conv_ln_b[layer], pw_b)
        na_l = _na_latent(p_l, p_c, na_rpb[layer])
        x = _matmul_residual([ret_l, conv_l, na_l], w_out_b, x, mod, layer, lat_row, 2, "out_proj")

        x = _ffn(x, norm2_g[layer], mod, layer, lat_row, up_b, ffn_dw_w[layer], ffn_dw_b[layer], down_b, final_g,
                 last, l, "ffn")

        if not last:
            conv_c = _conv_module(p_c, conv_dw_w[layer], conv_dw_b[layer], conv_ln_g[layer], conv_ln_b[layer], pw_b)
            na_c = _na_context(p_c)
            h_ctx = _matmul_residual([flat(ret_c), flat(conv_c), flat(na_c)], w_out_b, h_ctx, mod, layer, ctx_row, 2,
                                     "out_proj_ctx")
            h_ctx = _ffn(h_ctx, norm2_g[layer], mod, layer, ctx_row, up_b, ffn_dw_w[layer], ffn_dw_b[layer], down_b,
                         final_g, False, lc, "ffn_ctx")

    return x
```

```python
import functools

import numpy as np
import jax
import jax.numpy as jnp
from jax import lax
from jax.experimental import pallas as pl
from jax.experimental.pallas import tpu as pltpu

F32 = jnp.float32
BF16 = jnp.bfloat16

GRID_W = 64
RET_HEADS = 4
RET_DK = 128
RET_DV = 256
RET_CHUNK = 128
RET_UNROLL = 8
CONV_W = 512
CONV_K = 31
NA_HEADS = 4
NA_DH = 128
NA_ROWS = 8
NA_COLS = 16
FFN_K = 3
ROPE_BASE = 10000.0
EPS = 1e-6

RET_QK_W = RET_HEADS * RET_DK
RET_W = RET_HEADS * RET_DV
NA_W = NA_HEADS * NA_DH
OFF_RQ = 0
OFF_RK = OFF_RQ + RET_QK_W
OFF_RV = OFF_RK + RET_QK_W
OFF_RG = OFF_RV + RET_W
OFF_CA = OFF_RG + RET_W
OFF_CB = OFF_CA + CONV_W
OFF_NQ = OFF_CB + CONV_W
OFF_NK = OFF_NQ + NA_W
OFF_NV = OFF_NK + NA_W

V7X_VMEM_BYTES = 64 * 1024 * 1024
VMEM_LIMIT = V7X_VMEM_BYTES - 8 * 1024 * 1024
SUBLANES = 8
LANES = 128
COND_ROWS = 8
MASK_VALUE = -1e30
CONV_PAD = 16
CONV_TILE = 64


def _sigmoid(x):
    return 1.0 / (1.0 + jnp.exp(-x))


def _params(*sem):
    return pltpu.CompilerParams(dimension_semantics=sem, vmem_limit_bytes=VMEM_LIMIT)


def _adaln_kernel(c_ref, w_ref, b_ref, o_ref):
    c = c_ref[...]
    s = (c * _sigmoid(c)).astype(BF16)
    o_ref[0] = jnp.dot(s, w_ref[0].astype(BF16), preferred_element_type=F32) + b_ref[0]


def _adaln(cond, w_ada, b_ada):
    depth, d, n = w_ada.shape
    tn = 1024
    return pl.pallas_call(
        _adaln_kernel,
        grid=(depth, n // tn),
        in_specs=[
            pl.BlockSpec((COND_ROWS, d), lambda l, j: (0, 0)),
            pl.BlockSpec((1, d, tn), lambda l, j: (l, 0, j)),
            pl.BlockSpec((1, 1, tn), lambda l, j: (l, 0, j)),
        ],
        out_specs=pl.BlockSpec((1, COND_ROWS, tn), lambda l, j: (l, 0, j)),
        out_shape=jax.ShapeDtypeStruct((depth, COND_ROWS, n), F32),
        compiler_params=_params("parallel", "parallel"),
        name="adaln",
    )(cond, w_ada, b_ada.reshape(depth, 1, n))


def _mod_spec(layer, d, chunk, row):
    return pl.BlockSpec((None, 1, 1, d), lambda b, i, j: (layer, row(b), 0, chunk))


def _norm_mod_matmul_kernel(x_ref, g_ref, sh_ref, sc_ref, w_ref, o_ref, h_ref):
    @pl.when(pl.program_id(2) == 0)
    def _():
        x = x_ref[0]
        ms = jnp.mean(x * x, axis=-1, keepdims=True)
        gain = g_ref[...] * (1.0 + sc_ref[0])
        h_ref[...] = (x * lax.rsqrt(ms + EPS) * gain + sh_ref[0]).astype(BF16)

    o_ref[0] = jnp.dot(h_ref[...], w_ref[...], preferred_element_type=F32).astype(o_ref.dtype)


def _norm_mod_matmul(x, g, mod, layer, row, shift_chunk, scale_chunk, w, tn, name):
    b, l, d = x.shape
    n = w.shape[-1]
    tm = min(l, 1024)
    return pl.pallas_call(
        _norm_mod_matmul_kernel,
        grid=(b, l // tm, n // tn),
        in_specs=[
            pl.BlockSpec((1, tm, d), lambda b, i, j: (b, i, 0)),
            pl.BlockSpec((1, d), lambda b, i, j: (0, 0)),
            _mod_spec(layer, d, shift_chunk, row),
            _mod_spec(layer, d, scale_chunk, row),
            pl.BlockSpec((None, d, tn), lambda b, i, j: (layer, 0, j)),
        ],
        out_specs=pl.BlockSpec((1, tm, tn), lambda b, i, j: (b, i, j)),
        out_shape=jax.ShapeDtypeStruct((b, l, n), BF16),
        scratch_shapes=[pltpu.VMEM((tm, d), BF16)],
        compiler_params=_params("parallel", "parallel", "arbitrary"),
        name=name,
    )(x, g.reshape(1, d), mod, mod, w)


def _matmul_residual_kernel(*refs, n_a):
    a_refs = refs[:n_a]
    w_ref, res_ref, gate_ref, o_ref = refs[n_a:]
    acc = None
    off = 0
    for a_ref in a_refs:
        k = a_ref.shape[-1]
        part = jnp.dot(a_ref[0], w_ref[off:off + k, :], preferred_element_type=F32)
        acc = part if acc is None else acc + part
        off += k
    o_ref[0] = res_ref[0] + gate_ref[0] * acc


def _matmul_residual(a_list, w, res, mod, layer, row, gate_chunk, name):
    b, l, n = res.shape
    k_total = w.shape[1]
    tm = min(l, 512)
    tn = n
    in_specs = [pl.BlockSpec((1, tm, a.shape[-1]), lambda b, i, j: (b, i, 0)) for a in a_list]
    in_specs += [
        pl.BlockSpec((None, k_total, tn), lambda b, i, j: (layer, 0, j)),
        pl.BlockSpec((1, tm, tn), lambda b, i, j: (b, i, j)),
        pl.BlockSpec((None, 1, 1, tn), lambda b, i, j: (layer, row(b), 0, gate_chunk * (n // tn) + j)),
    ]
    return pl.pallas_call(
        functools.partial(_matmul_residual_kernel, n_a=len(a_list)),
        grid=(b, l // tm, n // tn),
        in_specs=in_specs,
        out_specs=pl.BlockSpec((1, tm, tn), lambda b, i, j: (b, i, j)),
        out_shape=jax.ShapeDtypeStruct((b, l, n), F32),
        compiler_params=_params("parallel", "parallel", "arbitrary"),
        name=name,
    )(*a_list, w, res, mod)


def _rope(t, cos, sin):
    lane = lax.broadcasted_iota(jnp.int32, t.shape, 1)
    partner = jnp.where((lane & 32) == 0, pltpu.roll(t, LANES - 32, 1), pltpu.roll(t, 32, 1))
    return t * cos + partner * sin


def _retention_kernel(lg_ref, ql_ref, kl_ref, vl_ref, gl_ref, qc_ref, kc_ref, vc_ref, gc_ref, cos_ref, sin_ref,
                      gn_ref, *rest, with_ctx, n_lat, n_ctx):
    if with_ctx:
        ol_ref, oc_ref, acc_ref, qr_ref, kr_ref, accc_ref = rest
    else:
        ol_ref, acc_ref, qr_ref, kr_ref = rest
    c = RET_CHUNK
    scale = RET_DK ** -0.5
    h = pl.program_id(1)
    lgf = lg_ref[0, h]
    lgb = lg_ref[1, h]
    ii = lax.broadcasted_iota(jnp.int32, (c, c), 0).astype(F32)
    jj = lax.broadcasted_iota(jnp.int32, (c, c), 1).astype(F32)
    diff = ii - jj
    decay = (jnp.where(diff >= 0, jnp.exp(jnp.maximum(diff, 0.0) * lgf), 0.0)
             + jnp.where(diff <= 0, jnp.exp(jnp.maximum(-diff, 0.0) * lgb), 0.0))
    ic = lax.broadcasted_iota(jnp.int32, (c, 1), 0).astype(F32)
    xi_f = jnp.exp((ic + 1.0) * lgf)
    zeta_f = jnp.exp((c - 1.0 - ic) * lgf)
    xi_b = jnp.exp((c - ic) * lgb)
    zeta_b = jnp.exp(ic * lgb)
    g_f = jnp.exp(jnp.full((1, RET_DV), float(c), F32) * lgf)
    g_b = jnp.exp(jnp.full((1, RET_DV), float(c), F32) * lgb)
    gn = gn_ref[...]

    def intra_and_inter(qb, kb, v, xi, st):
        s = lax.dot_general(qb, kb, (((1,), (1,)), ((), ())), preferred_element_type=F32)
        p = (s * decay).astype(BF16)
        inter = jnp.dot(qb, st.astype(BF16), preferred_element_type=F32) * xi
        return jnp.dot(p, v, preferred_element_type=F32) + inter

    def next_state(st, k_f32, v, zeta, g):
        kz = (k_f32 * zeta).T.astype(BF16)
        return st * g + jnp.dot(kz, v, preferred_element_type=F32)

    def finish(total, gate):
        mu = jnp.mean(total, axis=-1, keepdims=True)
        d = total - mu
        var = jnp.mean(d * d, axis=-1, keepdims=True)
        y = d * lax.rsqrt(var + EPS) * gn
        return (y * (gate * _sigmoid(gate))).astype(BF16)

    st = jnp.zeros((RET_DK, RET_DV), F32)
    for j in range(n_ctx):
        rows = slice(j * c, (j + 1) * c)
        q = qc_ref[0, rows, :].astype(F32) * scale
        k = kc_ref[0, rows, :].astype(F32)
        v = vc_ref[0, rows, :]
        if with_ctx:
            accc_ref[rows, :] = intra_and_inter(q.astype(BF16), k.astype(BF16), v, xi_f, st)
        st = next_state(st, k, v, zeta_f, g_f)

    nt = (((1,), (1,)), ((), ()))
    group = RET_UNROLL
    assert n_lat % group == 0

    def fwd_body(jg, st):
        rows, qb, kb, v, kzv = [], [], [], [], []
        for u in range(group):
            r = pl.ds(pl.multiple_of((jg * group + u) * c, c), c)
            cos = cos_ref[r, :]
            sin = sin_ref[r, :]
            q = _rope(ql_ref[0, r, :].astype(F32), cos, sin) * scale
            k = _rope(kl_ref[0, r, :].astype(F32), cos, sin)
            rows.append(r)
            qb.append(q.astype(BF16))
            kb.append(k.astype(BF16))
            v.append(vl_ref[0, r, :])
            qr_ref[r, :] = qb[u]
            kr_ref[r, :] = kb[u]
            kzv.append(jnp.dot((k * zeta_f).T.astype(BF16), v[u], preferred_element_type=F32))
        s = [lax.dot_general(qb[u], kb[u], nt, preferred_element_type=F32) for u in range(group)]
        states = []
        for u in range(group):
            states.append(st.astype(BF16))
            st = st * g_f + kzv[u]
        p = [(s[u] * decay).astype(BF16) for u in range(group)]
        for u in range(group):
            inter = jnp.dot(qb[u], states[u], preferred_element_type=F32) * xi_f
            acc_ref[rows[u], :] = jnp.dot(p[u], v[u], preferred_element_type=F32) + inter
        return st

    lax.fori_loop(0, n_lat // group, fwd_body, st)

    st = jnp.zeros((RET_DK, RET_DV), F32)
    for j in reversed(range(n_ctx)):
        rows = slice(j * c, (j + 1) * c)
        k = kc_ref[0, rows, :].astype(F32)
        v = vc_ref[0, rows, :]
        if with_ctx:
            qb = (qc_ref[0, rows, :].astype(F32) * scale).astype(BF16)
            inter = jnp.dot(qb, st.astype(BF16), preferred_element_type=F32) * xi_b
            oc_ref[0, rows, :] = finish(accc_ref[rows, :] + inter, gc_ref[0, rows, :].astype(F32))
        st = next_state(st, k, v, zeta_b, g_b)

    def bwd_body(jg, st):
        rows, kzv = [], []
        for u in range(group):
            r = pl.ds(pl.multiple_of((n_lat - 1 - (jg * group + u)) * c, c), c)
            rows.append(r)
            kz = (kr_ref[r, :].astype(F32) * zeta_b).T.astype(BF16)
            kzv.append(jnp.dot(kz, vl_ref[0, r, :], preferred_element_type=F32))
        states = []
        for u in range(group):
            states.append(st.astype(BF16))
            st = st * g_b + kzv[u]
        inter = [jnp.dot(qr_ref[rows[u], :], states[u], preferred_element_type=F32) * xi_b for u in range(group)]
        for u in range(group):
            ol_ref[0, rows[u], :] = finish(acc_ref[rows[u], :] + inter[u], gl_ref[0, rows[u], :].astype(F32))
        return st

    lax.fori_loop(0, n_lat // group, bwd_body, st)


def _retention(p_l, p_c, log_gamma, gn_g, cos, sin, with_ctx):
    b, l, _ = p_l.shape
    lc = p_c.shape[1]
    c = RET_CHUNK

    def col(width, off, length):
        return pl.BlockSpec((1, length, width), lambda b, h: (b, 0, off // width + h))

    in_specs = [
        pl.BlockSpec(memory_space=pltpu.SMEM),
        col(RET_DK, OFF_RQ, l), col(RET_DK, OFF_RK, l), col(RET_DV, OFF_RV, l), col(RET_DV, OFF_RG, l),
        col(RET_DK, OFF_RQ, lc), col(RET_DK, OFF_RK, lc), col(RET_DV, OFF_RV, lc), col(RET_DV, OFF_RG, lc),
        pl.BlockSpec((l, RET_DK), lambda b, h: (0, 0)),
        pl.BlockSpec((l, RET_DK), lambda b, h: (0, 0)),
        pl.BlockSpec((1, RET_DV), lambda b, h: (0, h)),
    ]
    out_specs = [pl.BlockSpec((1, l, RET_DV), lambda b, h: (b, 0, h))]
    out_shape = [jax.ShapeDtypeStruct((b, l, RET_W), BF16)]
    scratch = [pltpu.VMEM((l, RET_DV), F32), pltpu.VMEM((l, RET_DK), BF16), pltpu.VMEM((l, RET_DK), BF16)]
    if with_ctx:
        out_specs.append(pl.BlockSpec((1, lc, RET_DV), lambda b, h: (b, 0, h)))
        out_shape.append(jax.ShapeDtypeStruct((b, lc, RET_W), BF16))
        scratch.append(pltpu.VMEM((lc, RET_DV), F32))
    outs = pl.pallas_call(
        functools.partial(_retention_kernel, with_ctx=with_ctx, n_lat=l // c, n_ctx=lc // c),
        grid=(b, RET_HEADS),
        in_specs=in_specs,
        out_specs=out_specs,
        out_shape=out_shape,
        scratch_shapes=scratch,
        compiler_params=_params("parallel", "parallel"),
        name="retention_ctx" if with_ctx else "retention",
    )(log_gamma, p_l, p_l, p_l, p_l, p_c, p_c, p_c, p_c, cos, sin, gn_g.reshape(1, RET_W))
    return (outs[0], outs[1]) if with_ctx else (outs[0], None)


def _conv_kernel(a_ref, b_ref, dww_ref, dwb_ref, lng_ref, lnb_ref, pw_ref, o_ref, upad_ref, act_ref, *, l):
    pad = CONV_PAD
    upad_ref[0:pad, :] = jnp.zeros((pad, CONV_W), F32)
    upad_ref[pad + l:pad + l + pad, :] = jnp.zeros((pad, CONV_W), F32)
    tg = min(l, 256)

    def glu_body(t, carry):
        r0 = pl.multiple_of(t * tg, tg)
        a = a_ref[0, pl.ds(r0, tg), :].astype(F32)
        b = b_ref[0, pl.ds(r0, tg), :].astype(F32)
        upad_ref[pl.ds(pad + r0, tg), :] = a * _sigmoid(b)
        return carry

    lax.fori_loop(0, l // tg, glu_body, 0)

    tile = CONV_TILE
    halo = 2 * pad
    lng = lng_ref[...]
    lnb = lnb_ref[...]

    def conv_body(t, carry):
        r0 = pl.multiple_of(t * tile, tile)
        ys = []
        for cb in range(CONV_W // LANES):
            cols = slice(cb * LANES, (cb + 1) * LANES)
            win = upad_ref[pl.ds(r0, tile + halo), cols]
            acc = jnp.zeros((tile, LANES), F32) + dwb_ref[:, cols]
            for r in range(SUBLANES):
                rolled = win if r == 0 else pltpu.roll(win, tile + halo - r, 0)
                for a in range(halo // SUBLANES):
                    k = SUBLANES * a + r - 1
                    if 0 <= k < CONV_K:
                        acc = acc + rolled[SUBLANES * a:SUBLANES * a + tile] * dww_ref[k:k + 1, cols]
            ys.append(acc)
        y = jnp.concatenate(ys, axis=-1)
        mu = jnp.mean(y, axis=-1, keepdims=True)
        d = y - mu
        var = jnp.mean(d * d, axis=-1, keepdims=True)
        u = d * lax.rsqrt(var + EPS) * lng + lnb
        act_ref[pl.ds(r0, tile), :] = (u * _sigmoid(u)).astype(BF16)
        return carry

    lax.fori_loop(0, l // tile, conv_body, 0)

    tmm = min(l, 512)

    def mm_body(t, carry):
        r0 = pl.multiple_of(t * tmm, tmm)
        o_ref[0, pl.ds(r0, tmm), :] = jnp.dot(act_ref[pl.ds(r0, tmm), :], pw_ref[...],
                                              preferred_element_type=F32).astype(BF16)
        return carry

    lax.fori_loop(0, l // tmm, mm_body, 0)


def _conv_module(p, dw_w, dw_b, ln_g, ln_b, pw):
    b, l, _ = p.shape
    vec = pl.BlockSpec((1, CONV_W), lambda b: (0, 0))
    return pl.pallas_call(
        functools.partial(_conv_kernel, l=l),
        grid=(b,),
        in_specs=[
            pl.BlockSpec((1, l, CONV_W), lambda b: (b, 0, OFF_CA // CONV_W)),
            pl.BlockSpec((1, l, CONV_W), lambda b: (b, 0, OFF_CB // CONV_W)),
            pl.BlockSpec((CONV_K, CONV_W), lambda b: (0, 0)),
            vec, vec, vec,
            pl.BlockSpec((CONV_W, CONV_W), lambda b: (0, 0)),
        ],
        out_specs=pl.BlockSpec((1, l, CONV_W), lambda b: (b, 0, 0)),
        out_shape=jax.ShapeDtypeStruct((b, l, CONV_W), BF16),
        scratch_shapes=[pltpu.VMEM((l + 2 * CONV_PAD, CONV_W), F32),
                        pltpu.VMEM((l, CONV_W), BF16)],
        compiler_params=_params("parallel"),
        name="conv_module",
    )(p, p, dw_w, dw_b.reshape(1, CONV_W), ln_g.reshape(1, CONV_W), ln_b.reshape(1, CONV_W), pw)


NA_RPB_ROWS = 2 * NA_ROWS - 1
NA_RPB_COLS = 2 * NA_COLS - 1
NA_PAIRS = NA_RPB_ROWS - 1
NA_QROWS = 4
NA_UNION = NA_ROWS + NA_QROWS
NA_GROUP = 2


def _na_kernel(rpb_ref, q_ref, k_ref, v_ref, kc_ref, vc_ref, o_ref, pairs_ref, *, rows_n, kh):
    scale = NA_DH ** -0.5
    h = pl.program_id(0)
    nt = (((1,), (1,)), ((), ()))

    @pl.when(pl.program_id(1) == 0)
    def _build_bias():
        lane = lax.broadcasted_iota(jnp.int32, (GRID_W, 2 * GRID_W), 1)
        qcol = lax.broadcasted_iota(jnp.int32, (GRID_W, 2 * GRID_W), 0)
        w = lane & (GRID_W - 1)
        col_off = jnp.clip(w - qcol + (NA_COLS - 1), 0, NA_RPB_COLS - 1)
        col_start = jnp.clip(qcol - NA_COLS // 2, 0, GRID_W - NA_COLS)
        col_in = (w >= col_start) & (w < col_start + NA_COLS)
        left = lane < GRID_W

        def pair_body(k, carry):
            row = h * NA_RPB_ROWS + k
            acc = jnp.zeros((GRID_W, 2 * GRID_W), F32)
            for d in range(NA_RPB_COLS):
                val = jnp.where(left, rpb_ref[row, d], rpb_ref[row + 1, d])
                acc = jnp.where(col_off == d, val, acc)
            pairs_ref[k] = jnp.where(col_in, acc, MASK_VALUE)
            return carry

        lax.fori_loop(0, NA_PAIRS, pair_body, 0)

    kc = kc_ref[0]
    vc = vc_ref[0]

    lane = lax.broadcasted_iota(jnp.int32, (GRID_W, 2 * GRID_W), 1)
    left = lane < GRID_W
    nq = NA_QROWS * GRID_W
    nk = NA_UNION * GRID_W

    def scores(g):
        r0 = g * NA_QROWS
        u0 = jnp.clip(r0 - kh // 2, 0, rows_n - NA_UNION)
        q0 = pl.multiple_of(r0 * GRID_W, nq)
        k0 = pl.multiple_of(u0 * GRID_W, GRID_W)
        q = q_ref[0, pl.ds(q0, nq), :]
        kw = k_ref[0, pl.ds(k0, nk), :]
        vw = v_ref[0, pl.ds(k0, nk), :]
        bias_rows = []
        for rq in range(NA_QROWS):
            r = r0 + rq
            s0 = jnp.clip(r - kh // 2, 0, rows_n - kh)
            tiles = []
            for m in range(NA_UNION // 2):
                kr = u0 + 2 * m
                ok_l = (kr >= s0) & (kr < s0 + kh)
                ok_r = (kr + 1 >= s0) & (kr + 1 < s0 + kh)
                ok = jnp.where(left, ok_l.astype(jnp.int32), ok_r.astype(jnp.int32)) != 0
                pair = pairs_ref[jnp.clip(kr - r + (NA_ROWS - 1), 0, NA_PAIRS - 1)]
                tiles.append(jnp.where(ok, pair, MASK_VALUE))
            bias_rows.append(jnp.concatenate(tiles, axis=-1))
        bias = jnp.concatenate(bias_rows, axis=0)
        s_lat = lax.dot_general(q, kw, nt, preferred_element_type=F32) * scale + bias
        s_ctx = lax.dot_general(q, kc, nt, preferred_element_type=F32) * scale
        return s_lat, s_ctx, vw, q0

    def weights(s_lat, s_ctx):
        m = jnp.maximum(jnp.max(s_lat, axis=-1, keepdims=True), jnp.max(s_ctx, axis=-1, keepdims=True))
        e_lat = jnp.exp(s_lat - m)
        e_ctx = jnp.exp(s_ctx - m)
        den = jnp.sum(e_lat, axis=-1, keepdims=True) + jnp.sum(e_ctx, axis=-1, keepdims=True)
        return e_lat.astype(BF16), e_ctx.astype(BF16), den

    def body(t, carry):
        sc = [scores(t * NA_GROUP + u) for u in range(NA_GROUP)]
        ws = [weights(s[0], s[1]) for s in sc]
        for (s_lat, s_ctx, vw, q0), (e_lat, e_ctx, den) in zip(sc, ws):
            o = jnp.dot(e_lat, vw, preferred_element_type=F32) + jnp.dot(e_ctx, vc, preferred_element_type=F32)
            o_ref[0, pl.ds(q0, nq), :] = (o / den).astype(BF16)
        return carry

    lax.fori_loop(0, rows_n // (NA_QROWS * NA_GROUP), body, 0)


def _na_latent(p_l, p_c, rpb):
    b, l, _ = p_l.shape
    lc = p_c.shape[1]
    rows_n = l // GRID_W
    kh = NA_ROWS
    assert rpb.shape == (NA_HEADS, NA_RPB_ROWS, NA_RPB_COLS)
    assert rows_n >= NA_UNION and rows_n % (NA_QROWS * NA_GROUP) == 0 and (rows_n - NA_UNION) % 2 == 0

    def col(off, length):
        return pl.BlockSpec((1, length, NA_DH), lambda h, b: (b, 0, off // NA_DH + h))

    return pl.pallas_call(
        functools.partial(_na_kernel, rows_n=rows_n, kh=kh),
        grid=(NA_HEADS, b),
        in_specs=[pl.BlockSpec(memory_space=pltpu.SMEM),
                  col(OFF_NQ, l), col(OFF_NK, l), col(OFF_NV, l), col(OFF_NK, lc), col(OFF_NV, lc)],
        out_specs=pl.BlockSpec((1, l, NA_DH), lambda h, b: (b, 0, h)),
        out_shape=jax.ShapeDtypeStruct((b, l, NA_W), BF16),
        scratch_shapes=[pltpu.VMEM((NA_PAIRS, GRID_W, 2 * GRID_W), F32)],
        compiler_params=_params("parallel", "arbitrary"),
        name="na_latent",
    )(rpb.astype(F32).reshape(NA_HEADS * NA_RPB_ROWS, NA_RPB_COLS), p_l, p_l, p_l, p_c, p_c)


def _na_ctx_kernel(q_ref, k_ref, v_ref, o_ref):
    scale = NA_DH ** -0.5
    s = lax.dot_general(q_ref[0], k_ref[0], (((1,), (1,)), ((), ())), preferred_element_type=F32) * scale
    e = jnp.exp(s - jnp.max(s, axis=-1, keepdims=True))
    o = jnp.dot(e.astype(BF16), v_ref[0], preferred_element_type=F32)
    o_ref[0] = (o / jnp.sum(e, axis=-1, keepdims=True)).astype(BF16)


def _na_context(p_c):
    b, lc, _ = p_c.shape

    def col(off):
        return pl.BlockSpec((1, lc, NA_DH), lambda b, h: (b, 0, off // NA_DH + h))

    return pl.pallas_call(
        _na_ctx_kernel,
        grid=(b, NA_HEADS),
        in_specs=[col(OFF_NQ), col(OFF_NK), col(OFF_NV)],
        out_specs=pl.BlockSpec((1, lc, NA_DH), lambda b, h: (b, 0, h)),
        out_shape=jax.ShapeDtypeStruct((b, lc, NA_W), BF16),
        compiler_params=_params("parallel", "parallel"),
        name="na_context",
    )(p_c, p_c, p_c)


FFN_HALO = 16
FFN_SLICE = 256
FFN_ROWS = 1024
def _ffn_kernel(x_ref, xp_ref, xn_ref, g_ref, sh_ref, sc_ref, gate_ref, upv_ref, upg_ref, cw_ref, down_ref, fg_ref,
                o_ref, h_ref, *, final_norm, seq):
    i = pl.program_id(1)
    j = pl.program_id(2)
    tm = x_ref.shape[1]

    def norm_mod(x):
        ms = jnp.mean(x * x, axis=-1, keepdims=True)
        gain = g_ref[...] * (1.0 + sc_ref[0])
        return x * lax.rsqrt(ms + EPS) * gain + sh_ref[0]

    @pl.when(j == 0)
    def _():
        h_ref[0:tm, :] = norm_mod(x_ref[0]).astype(BF16)
        hp = jnp.where(i == 0, 0.0, norm_mod(xp_ref[0]))
        hn = jnp.where(i == pl.num_programs(1) - 1, 0.0, norm_mod(xn_ref[0]))
        h_ref[tm:tm + FFN_HALO, :] = jnp.concatenate([hp, hn], axis=0).astype(BF16)
        o_ref[0] = jnp.zeros(o_ref.shape[1:], F32)

    h = h_ref[...]
    edge = lax.broadcasted_iota(jnp.int32, (SUBLANES, 1), 0)

    def conv3(u, t0, b0, c):
        x = u[0:tm]
        prev_row = u[tm + SUBLANES - 1:tm + SUBLANES]
        next_row = u[tm + SUBLANES:tm + SUBLANES + 1]
        x_prev = pltpu.roll(x, 1, 0)
        x_next = pltpu.roll(x, tm - 1, 0)
        if seq < tm:
            pos = lax.rem(lax.broadcasted_iota(jnp.int32, (tm, 1), 0), seq)
            x_prev = jnp.where(pos == 0, 0.0, x_prev)
            x_next = jnp.where(pos == seq - 1, 0.0, x_next)
        else:
            x_prev = jnp.concatenate([jnp.where(edge == 0, prev_row, x_prev[0:SUBLANES]), x_prev[SUBLANES:]],
                                     axis=0)
            x_next = jnp.concatenate([x_next[:tm - SUBLANES],
                                      jnp.where(edge == SUBLANES - 1, next_row, x_next[tm - SUBLANES:])], axis=0)
        return (x_prev * cw_ref[t0:t0 + 1, c] + x * cw_ref[t0 + 1:t0 + 2, c] + x_next * cw_ref[t0 + 2:t0 + 3, c]
                + cw_ref[b0:b0 + 1, c])

    tf = upv_ref.shape[1]
    width = min(tf, FFN_SLICE)
    slices = [slice(s * width, (s + 1) * width) for s in range(tf // width)]
    us = []
    for c in slices:
        us.append((jnp.dot(h, upv_ref[:, c], preferred_element_type=F32),
                   jnp.dot(h, upg_ref[:, c], preferred_element_type=F32)))
    for (u_val, u_gate), c in zip(us, slices):
        val = conv3(u_val, 0, 2 * FFN_K, c)
        gate = conv3(u_gate, FFN_K, 2 * FFN_K + 1, c)
        act = (gate * _sigmoid(gate) * val).astype(BF16)
        o_ref[0] += jnp.dot(act, down_ref[c, :], preferred_element_type=F32)

    @pl.when(j == pl.num_programs(2) - 1)
    def _():
        y = x_ref[0] + gate_ref[0] * o_ref[0]
        if final_norm:
            ms = jnp.mean(y * y, axis=-1, keepdims=True)
            y = y * lax.rsqrt(ms + EPS) * fg_ref[...]
        o_ref[0] = y


def _ffn(x, g, mod, layer, row, up, dw_w, dw_b, down, final_g, final_norm, seq, name):
    b, l, d = x.shape
    f = down.shape[1]
    tm = min(l, FFN_ROWS)
    assert seq == l or (seq < tm and tm % seq == 0)
    tf = min(f, 512)
    nf = f // tf
    rb = tm // SUBLANES

    def mspec(chunk):
        return pl.BlockSpec((None, 1, 1, d), lambda b, i, j: (layer, row(b), 0, chunk))

    cw = jnp.concatenate([dw_w[:, :f], dw_w[:, f:], dw_b[None, :f], dw_b[None, f:]], axis=0)

    in_specs = [
        pl.BlockSpec((1, tm, d), lambda b, i, j: (b, i, 0), pipeline_mode=pl.Buffered(1)),
        pl.BlockSpec((1, SUBLANES, d), lambda b, i, j: (b, jnp.maximum(i * rb - 1, 0), 0)),
        pl.BlockSpec((1, SUBLANES, d), lambda b, i, j: (b, jnp.minimum((i + 1) * rb, l // SUBLANES - 1), 0)),
        pl.BlockSpec((1, d), lambda b, i, j: (0, 0)),
        mspec(3), mspec(4), mspec(5),
        pl.BlockSpec((None, d, tf), lambda b, i, j: (layer, 0, j)),
        pl.BlockSpec((None, d, tf), lambda b, i, j: (layer, 0, nf + j)),
        pl.BlockSpec((2 * FFN_K + 2, tf), lambda b, i, j: (0, j)),
        pl.BlockSpec((None, tf, d), lambda b, i, j: (layer, j, 0)),
        pl.BlockSpec((1, d), lambda b, i, j: (0, 0)),
    ]
    return pl.pallas_call(
        functools.partial(_ffn_kernel, final_norm=final_norm, seq=seq),
        grid=(b, l // tm, nf),
        in_specs=in_specs,
        out_specs=pl.BlockSpec((1, tm, d), lambda b, i, j: (b, i, 0)),
        out_shape=jax.ShapeDtypeStruct((b, l, d), F32),
        scratch_shapes=[pltpu.VMEM((tm + FFN_HALO, d), BF16)],
        compiler_params=_params("parallel", "parallel", "arbitrary"),
        name=name,
    )(x, x, x, g.reshape(1, d), mod, mod, mod, up, up, cw, down, final_g.reshape(1, d))


def _rope_tables(l):
    nf = RET_DK // 4
    pos = np.arange(l)
    inv = ROPE_BASE ** (-jnp.arange(nf, dtype=F32) / nf)
    ang_r = jnp.asarray(pos // GRID_W, F32)[:, None] * inv[None, :]
    ang_c = jnp.asarray(pos % GRID_W, F32)[:, None] * inv[None, :]
    cos = jnp.concatenate([jnp.cos(ang_r), jnp.cos(ang_r), jnp.cos(ang_c), jnp.cos(ang_c)], axis=-1)
    sin = jnp.concatenate([-jnp.sin(ang_r), jnp.sin(ang_r), -jnp.sin(ang_c), jnp.sin(ang_c)], axis=-1)
    return cos, sin


def kernel(x, c, ctx, c_ctx, w_ada, b_ada, norm1_g, w_in, ret_decay, ret_gn_g, conv_dw_w, conv_dw_b, conv_ln_g,
           conv_ln_b, conv_pw, na_rpb, w_out, norm2_g, ffn_up, ffn_dw_w, ffn_dw_b, ffn_down, final_g):
    depth = w_ada.shape[0]
    bsz, l, d = x.shape

    cond = jnp.zeros((COND_ROWS, d), F32).at[:bsz].set(c).at[bsz].set(c_ctx)
    mod = _adaln(cond, w_ada, b_ada).reshape(depth, COND_ROWS, 1, 6 * d)
    lat_row = lambda b: b
    ctx_row = lambda b: bsz
    cos, sin = _rope_tables(l)
    log_gamma = jax.nn.log_sigmoid(ret_decay.astype(F32))
    w_in_b = w_in.astype(BF16)
    w_out_b = w_out.astype(BF16)
    up_b = ffn_up.astype(BF16)
    down_b = ffn_down.astype(BF16)
    tn_in = w_in.shape[-1] // 4

    lc = ctx.shape[1]
    flat = lambda t: t.reshape(1, bsz * lc, t.shape[-1])
    h_ctx = flat(ctx)
    for layer in range(depth):
        last = layer == depth - 1
        pw_b = conv_pw[layer].astype(BF16)

        p_l = _norm_mod_matmul(x, norm1_g[layer], mod, layer, lat_row, 0, 1, w_in_b, tn_in, "in_proj")
        p_c = _norm_mod_matmul(h_ctx, norm1_g[layer], mod, layer, ctx_row, 0, 1, w_in_b, tn_in, "in_proj_ctx")
        p_c = p_c.reshape(bsz, lc, p_c.shape[-1])

        ret_l, ret_c = _retention(p_l, p_c, log_gamma[layer], ret_gn_g[layer], cos, sin, with_ctx=not last)
        conv_l = _conv_module(p_l, conv_dw_w[layer], conv_dw_b[layer], conv_ln_g[layer], conv_ln_b[layer], pw_b)
        na_l = _na_latent(p_l, p_c, na_rpb[layer])
        x = _matmul_residual([ret_l, conv_l, na_l], w_out_b, x, mod, layer, lat_row, 2, "out_proj")

        x = _ffn(x, norm2_g[layer], mod, layer, lat_row, up_b, ffn_dw_w[layer], ffn_dw_b[layer], down_b, final_g,
                 last, l, "ffn")

        if not last:
            conv_c = _conv_module(p_c, conv_dw_w[layer], conv_dw_b[layer], conv_ln_g[layer], conv_ln_b[layer], pw_b)
            na_c = _na_context(p_c)
            h_ctx = _matmul_residual([flat(ret_c), flat(conv_c), flat(na_c)], w_out_b, h_ctx, mod, layer, ctx_row, 2,
                                     "out_proj_ctx")
            h_ctx = _ffn(h_ctx, norm2_g[layer], mod, layer, ctx_row, up_b, ffn_dw_w[layer], ffn_dw_b[layer], down_b,
                         final_g, False, lc, "ffn_ctx")

    return x
```
